```python
import jax, jax.numpy as jnp
from jax import lax
import numpy as np

D_MODEL = 1024
BATCH = 16
SEQ = 2048
DEPTH = 2
DEC_BATCH = 32
DEC_SEQ = 16
PAST_LEN = 2048

CHUNK = 64
D_MIX = D_MODEL
W_POOL = D_MIX // 4
POOL_WINDOWS = (2, 4, 8, 16)
N_POOL_GROUPS = len(POOL_WINDOWS)
POOL_GROUP = W_POOL // N_POOL_GROUPS
POOL_STATE = max(POOL_WINDOWS) - 1
W_SCONV = (D_MIX - W_POOL) // 2
W_CCONV = D_MIX - W_POOL - W_SCONV
SCONV_K = 3
CCONV_K = 31
D_FF = 2816
FFN_CONV_K = 3
D_PLE = 256
EPS = 1e-6
D_IN = W_POOL + 3 * W_SCONV + 2 * W_CCONV

kernel_name = "hybrid_pool_conv_streaming_step"


def rms_norm(x, g):
    x32 = x.astype(jnp.float32)
    y = x32 * lax.rsqrt(jnp.mean(x32 * x32, axis=-1, keepdims=True) + EPS)
    return (y * g.astype(jnp.float32)).astype(x.dtype)


def layer_norm(x, g, b):
    x32 = x.astype(jnp.float32)
    mu = jnp.mean(x32, axis=-1, keepdims=True)
    xc = x32 - mu
    y = xc * lax.rsqrt(jnp.mean(xc * xc, axis=-1, keepdims=True) + EPS)
    return (y * g.astype(jnp.float32) + b.astype(jnp.float32)).astype(x.dtype)


def causal_dwconv(x, prev, w, b):
    k = w.shape[0]
    xp = jnp.concatenate([prev.astype(x.dtype), x], axis=1)
    y = lax.conv_general_dilated(
        xp, w[:, None, :].astype(x.dtype), window_strides=(1,), padding='VALID',
        dimension_numbers=('NWC', 'WIO', 'NWC'), feature_group_count=x.shape[-1])
    return y + b.astype(x.dtype), xp[:, xp.shape[1] - (k - 1):]


def multiscale_pool(u, prev, pos0, w_grp, scale):
    bsz, L, c = u.shape
    up = jnp.concatenate([prev.astype(u.dtype), u], axis=1)
    cs = jnp.cumsum(up.astype(jnp.float32), axis=1)
    cs0 = jnp.concatenate([jnp.zeros((bsz, 1, c), jnp.float32), cs], axis=1)
    pos = pos0 + jnp.arange(L)
    outs = []
    for g, w in enumerate(POOL_WINDOWS):
        lo, hi = g * POOL_GROUP, (g + 1) * POOL_GROUP
        s = cs0[:, POOL_STATE + 1:, lo:hi] - cs0[:, POOL_STATE + 1 - w:POOL_STATE + 1 - w + L, lo:hi]
        cnt = jnp.minimum(pos + 1, w).astype(jnp.float32)
        outs.append(s / cnt[None, :, None])
    mean = jnp.concatenate(outs, axis=-1)
    diff = (mean - u.astype(jnp.float32)).astype(u.dtype)
    mixed = jnp.einsum('blgc,gcd->blgd', diff.reshape(bsz, L, N_POOL_GROUPS, POOL_GROUP), w_grp)
    return mixed.reshape(bsz, L, c) * scale, up[:, L:]


def trunk_layer(h, p, pos0, st_pool, st_sconv, st_cconv, st_ffn,
                g_mix, w_in, w_pool, pool_scale, sconv_w, sconv_b, cconv_w, cconv_b,
                ln_g, ln_b, w_out, g_ffn, w_up, ffn_conv_w, ffn_conv_b, w_down,
                g_ple, w_ple_gate, w_ple_proj):
    n = rms_norm(h, g_mix)
    u = n @ w_in
    o1 = W_POOL
    o2 = o1 + W_SCONV
    o3 = o2 + W_SCONV
    o4 = o3 + W_SCONV
    o5 = o4 + W_CCONV
    u_pool, b_g, c_g, v = u[..., :o1], u[..., o1:o2], u[..., o2:o3], u[..., o3:o4]
    glu_a, glu_g = u[..., o4:o5], u[..., o5:]
    y_pool, ns_pool = multiscale_pool(u_pool, st_pool, pos0, w_pool, pool_scale)
    sc, ns_sconv = causal_dwconv(c_g * v, st_sconv, sconv_w, sconv_b)
    y_sconv = b_g * sc
    glu = glu_a * jax.nn.sigmoid(glu_g)
    cc, ns_cconv = causal_dwconv(glu, st_cconv, cconv_w, cconv_b)
    y_cconv = jax.nn.silu(layer_norm(cc, ln_g, ln_b))
    h = h + jnp.concatenate([y_pool, y_sconv, y_cconv], axis=-1) @ w_out
    up = rms_norm(h, g_ffn) @ w_up
    upc, ns_ffn = causal_dwconv(up, st_ffn, ffn_conv_w, ffn_conv_b)
    gate, val = jnp.split(upc, 2, axis=-1)
    h = h + (jax.nn.silu(gate) * val) @ w_down
    h = h + (p @ w_ple_proj) * jax.nn.sigmoid(rms_norm(h, g_ple) @ w_ple_gate)
    return h, ns_pool, ns_sconv, ns_cconv, ns_ffn


def setup_inputs(seed: int = 0) -> dict:
    key = jax.random.key(seed)
    ks = jax.random.split(key, 32)
    f32 = jnp.float32
    nrm = lambda k, s, sc: jax.random.normal(k, s, f32) * sc
    gain = lambda k, s: 1.0 + 0.02 * jax.random.normal(k, s, f32)
    return {
        "x_prompt": nrm(ks[0], (BATCH, SEQ, D_MODEL), 1.0),
        "x_sample": nrm(ks[1], (DEC_BATCH, DEC_SEQ, D_MODEL), 1.0),
        "p_prompt": nrm(ks[2], (DEPTH, BATCH, SEQ, D_PLE), 1.0),
        "p_sample": nrm(ks[3], (DEPTH, DEC_BATCH, DEC_SEQ, D_PLE), 1.0),
        "state_pool": nrm(ks[4], (DEPTH, DEC_BATCH, POOL_STATE, W_POOL), 1.0),
        "state_sconv": nrm(ks[5], (DEPTH, DEC_BATCH, SCONV_K - 1, W_SCONV), 1.0),
        "state_cconv": nrm(ks[6], (DEPTH, DEC_BATCH, CCONV_K - 1, W_CCONV), 0.5),
        "state_ffn": nrm(ks[7], (DEPTH, DEC_BATCH, FFN_CONV_K - 1, 2 * D_FF), 1.0),
        "g_mix": gain(ks[8], (DEPTH, D_MODEL)),
        "w_in": nrm(ks[9], (DEPTH, D_MODEL, D_IN), D_MODEL ** -0.5),
        "w_pool": nrm(ks[10], (DEPTH, N_POOL_GROUPS, POOL_GROUP, POOL_GROUP), POOL_GROUP ** -0.5),
        "pool_scale": gain(ks[11], (DEPTH, W_POOL)),
        "sconv_w": nrm(ks[12], (DEPTH, SCONV_K, W_SCONV), SCONV_K ** -0.5),
        "sconv_b": nrm(ks[13], (DEPTH, W_SCONV), 0.01),
        "cconv_w": nrm(ks[14], (DEPTH, CCONV_K, W_CCONV), CCONV_K ** -0.5),
        "cconv_b": nrm(ks[15], (DEPTH, W_CCONV), 0.01),
        "ln_g": gain(ks[16], (DEPTH, W_CCONV)),
        "ln_b": nrm(ks[17], (DEPTH, W_CCONV), 0.01),
        "w_out": nrm(ks[18], (DEPTH, D_MIX, D_MODEL), D_MIX ** -0.5),
        "g_ffn": gain(ks[19], (DEPTH, D_MODEL)),
        "w_up": nrm(ks[20], (DEPTH, D_MODEL, 2 * D_FF), D_MODEL ** -0.5),
        "ffn_conv_w": nrm(ks[21], (DEPTH, FFN_CONV_K, 2 * D_FF), FFN_CONV_K ** -0.5),
        "ffn_conv_b": nrm(ks[22], (DEPTH, 2 * D_FF), 0.01),
        "w_down": nrm(ks[23], (DEPTH, D_FF, D_MODEL), D_FF ** -0.5),
        "g_ple": gain(ks[24], (DEPTH, D_MODEL)),
        "w_ple_gate": nrm(ks[25], (DEPTH, D_MODEL, D_MODEL), D_MODEL ** -0.5),
        "w_ple_proj": nrm(ks[26], (DEPTH, D_PLE, D_MODEL), D_PLE ** -0.5),
        "g_final": gain(ks[27], (D_MODEL,)),
    }


def reference(x_prompt, x_sample, p_prompt, p_sample, state_pool, state_sconv, state_cconv, state_ffn,
              g_mix, w_in, w_pool, pool_scale, sconv_w, sconv_b, cconv_w, cconv_b, ln_g, ln_b,
              w_out, g_ffn, w_up, ffn_conv_w, ffn_conv_b, w_down, g_ple, w_ple_gate, w_ple_proj,
              g_final):
    dt = x_prompt.dtype
    z_pool = jnp.zeros((BATCH, POOL_STATE, W_POOL), dt)
    z_sconv = jnp.zeros((BATCH, SCONV_K - 1, W_SCONV), dt)
    z_cconv = jnp.zeros((BATCH, CCONV_K - 1, W_CCONV), dt)
    z_ffn = jnp.zeros((BATCH, FFN_CONV_K - 1, 2 * D_FF), dt)
    hp, hs = x_prompt, x_sample
    pp_l, psc_l, pcc_l, pf_l = [], [], [], []
    sp_l, ssc_l, scc_l, sf_l = [], [], [], []
    for i in range(DEPTH):
        lw = (g_mix[i], w_in[i], w_pool[i], pool_scale[i], sconv_w[i], sconv_b[i], cconv_w[i],
              cconv_b[i], ln_g[i], ln_b[i], w_out[i], g_ffn[i], w_up[i], ffn_conv_w[i],
              ffn_conv_b[i], w_down[i], g_ple[i], w_ple_gate[i], w_ple_proj[i])
        hp, a, b, c, d = trunk_layer(hp, p_prompt[i], 0, z_pool, z_sconv, z_cconv, z_ffn, *lw)
        pp_l.append(a); psc_l.append(b); pcc_l.append(c); pf_l.append(d)
        hs, a, b, c, d = trunk_layer(hs, p_sample[i], PAST_LEN, state_pool[i], state_sconv[i],
                                     state_cconv[i], state_ffn[i], *lw)
        sp_l.append(a); ssc_l.append(b); scc_l.append(c); sf_l.append(d)
    y_prompt = rms_norm(hp, g_final)
    y_sample = rms_norm(hs, g_final)
    return (y_prompt, y_sample,
            jnp.stack(pp_l), jnp.stack(psc_l), jnp.stack(pcc_l), jnp.stack(pf_l),
            jnp.stack(sp_l), jnp.stack(ssc_l), jnp.stack(scc_l), jnp.stack(sf_l))
```

```python
import functools

import jax
import jax.numpy as jnp
from jax import lax
from jax.experimental import pallas as pl
from jax.experimental.pallas import tpu as pltpu

D_MODEL = 1024
W_POOL = 256
POOL_WINDOWS = (2, 4, 8, 16)
POOL_GROUP = W_POOL // len(POOL_WINDOWS)
POOL_STATE = max(POOL_WINDOWS) - 1
W_SCONV = 384
W_CCONV = 384
SCONV_K = 3
CCONV_K = 31
D_FF = 2816
FFN_CONV_K = 3
D_PLE = 256
PAST_LEN = 2048
EPS = 1e-6
D_IN = W_POOL + 3 * W_SCONV + 2 * W_CCONV

O_B = W_POOL
O_C = O_B + W_SCONV
O_V = O_C + W_SCONV
O_GA = O_V + W_SCONV
O_GG = O_GA + W_CCONV

V7X_SUBLANES = 8
V7X_LANES = 128
V7X_MXU_DIM = 256
V7X_VMEM_BYTES = 64 * 1024 * 1024

H_POOL = 16
H_SCONV = 8
H_CCONV = 32
H_FFN = 8

FFN_CHUNK = V7X_MXU_DIM
N_FFN_CHUNKS = D_FF // FFN_CHUNK
ROW_CHUNK = 64


def _sigmoid(x):
    return 1.0 / (1.0 + jnp.exp(-x))


def _rms(x, g_ref):
    ms = jnp.mean(x * x, axis=-1, keepdims=True)
    return x * lax.rsqrt(ms + EPS) * g_ref[...]


def _dot(a, b):
    return jnp.dot(a, b, preferred_element_type=jnp.float32)


def _layer_kernel(h_ref, p_ref, stp_ref, sts_ref, stc_ref, stf_ref,
                  gmix_ref, win_ref, wpool_ref, pscale_ref, scw_ref, scb_ref,
                  ccw_ref, ccb_ref, lng_ref, lnb_ref, wout_ref, gffn_ref, wup_ref,
                  fcw_ref, fcb_ref, wdown_ref, gple_ref, wgate_ref, wproj_ref, gfin_ref,
                  ho_ref, nsp_ref, nss_ref, nsc_ref, nsf_ref,
                  pool_buf, sconv_buf, cconv_buf, ffn_buf, ffn_carry, mix_buf, act_buf,
                  *, bb, t, n_seq, pos0, final):
    m = bb * t
    s = pl.program_id(1)

    if bb == 1:
        chunks = [(0, 1, r, r + ROW_CHUNK, r, r + ROW_CHUNK) for r in range(0, t, ROW_CHUNK)]
    else:
        bc = ROW_CHUNK // t
        chunks = [(b, b + bc, 0, t, b * t, (b + bc) * t) for b in range(0, bb, bc)]

    def rd(buf, ch, off):
        b0, b1, t0, t1, r0, r1 = ch
        v = buf[b0:b1, off + t0:off + t1, :]
        return v.reshape(r1 - r0, v.shape[-1])

    def rdc(buf, ch, off, c0, c1):
        b0, b1, t0, t1, r0, r1 = ch
        v = buf[b0:b1, off + t0:off + t1, c0:c1]
        return v.reshape(r1 - r0, c1 - c0)

    @pl.when(s == 0)
    def _():
        pool_buf[:, H_POOL - POOL_STATE:H_POOL, :] = stp_ref[...]
        sconv_buf[:, H_SCONV - (SCONV_K - 1):H_SCONV, :] = sts_ref[...]
        cconv_buf[:, H_CCONV - (CCONV_K - 1):H_CCONV, :] = stc_ref[...]
        ffn_carry[:, H_FFN - (FFN_CONV_K - 1):H_FFN, :] = stf_ref[...]

    h = h_ref[...].reshape(m, D_MODEL)
    n1 = _rms(h, gmix_ref).astype(jnp.bfloat16)
    u = _dot(n1, win_ref[...])

    pool_buf[:, H_POOL:H_POOL + t, :] = u[:, 0:O_B].reshape(bb, t, W_POOL)
    sconv_buf[:, H_SCONV:H_SCONV + t, :] = (u[:, O_C:O_V] * u[:, O_V:O_GA]).reshape(bb, t, W_SCONV)
    cconv_buf[:, H_CCONV:H_CCONV + t, :] = (
        u[:, O_GA:O_GG] * _sigmoid(u[:, O_GG:D_IN])).reshape(bb, t, W_CCONV)

    lane = lax.broadcasted_iota(jnp.int32, (1, V7X_LANES), 1)
    low_half = lane < POOL_GROUP

    for ch in chunks:
        b0, b1, t0, t1, r0, r1 = ch
        rows = r1 - r0
        tpos = lax.broadcasted_iota(jnp.int32, (rows, V7X_LANES), 0)
        if bb > 1:
            tpos = lax.rem(tpos, t)
        avail = pos0 + s * t + t0 + tpos + 1
        pooled = []
        for half, (w_lo, w_hi) in enumerate(((2, 4), (8, 16))):
            c0, c1 = half * V7X_LANES, (half + 1) * V7X_LANES
            acc = rdc(pool_buf, ch, H_POOL, c0, c1)
            x_now = acc
            sum_lo = None
            for k in range(1, w_hi):
                acc = acc + rdc(pool_buf, ch, H_POOL - k, c0, c1)
                if k == w_lo - 1:
                    sum_lo = acc
            ssum = jnp.where(low_half, sum_lo, acc)
            win = jnp.where(low_half, w_lo, w_hi)
            cnt = jnp.minimum(avail, win).astype(jnp.float32)
            pooled.append(ssum / cnt - x_now)
        diff = jnp.concatenate(pooled, axis=-1).astype(jnp.bfloat16)
        y_pool = _dot(diff, wpool_ref[...]) * pscale_ref[...]
        mix_buf[r0:r1, 0:O_B] = y_pool.astype(jnp.bfloat16)

        acc = None
        for k in range(SCONV_K):
            term = rd(sconv_buf, ch, H_SCONV - (SCONV_K - 1) + k) * scw_ref[k:k + 1, :]
            acc = term if acc is None else acc + term
        y_sconv = u[r0:r1, O_B:O_C] * (acc + scb_ref[...])
        mix_buf[r0:r1, O_B:O_B + W_SCONV] = y_sconv.astype(jnp.bfloat16)

        acc = None
        for k in range(CCONV_K):
            term = rd(cconv_buf, ch, H_CCONV - (CCONV_K - 1) + k) * ccw_ref[k:k + 1, :]
            acc = term if acc is None else acc + term
        cc = acc + ccb_ref[...]
        mu = jnp.mean(cc, axis=-1, keepdims=True)
        xc = cc - mu
        ln = xc * lax.rsqrt(jnp.mean(xc * xc, axis=-1, keepdims=True) + EPS)
        ln = ln * lng_ref[...] + lnb_ref[...]
        y_cconv = ln * _sigmoid(ln)
        mix_buf[r0:r1, O_B + W_SCONV:D_MODEL] = y_cconv.astype(jnp.bfloat16)

    new_pool = pool_buf[:, t + H_POOL - POOL_STATE:t + H_POOL, :]
    new_sconv = sconv_buf[:, t + H_SCONV - (SCONV_K - 1):t + H_SCONV, :]
    new_cconv = cconv_buf[:, t + H_CCONV - (CCONV_K - 1):t + H_CCONV, :]
    nsp_ref[...] = new_pool
    nss_ref[...] = new_sconv
    nsc_ref[...] = new_cconv
    if n_seq > 1:
        pool_buf[:, H_POOL - POOL_STATE:H_POOL, :] = new_pool
        sconv_buf[:, H_SCONV - (SCONV_K - 1):H_SCONV, :] = new_sconv
        cconv_buf[:, H_CCONV - (CCONV_K - 1):H_CCONV, :] = new_cconv

    h1 = h + _dot(mix_buf[...], wout_ref[...])

    n2 = _rms(h1, gffn_ref).astype(jnp.bfloat16)
    for j in range(N_FFN_CHUNKS):
        par = j % 2
        cols = (j * FFN_CHUNK, D_FF + j * FFN_CHUNK)
        for gv in range(2):
            c0 = cols[gv]
            upj = _dot(n2, wup_ref[:, c0:c0 + FFN_CHUNK])
            ffn_buf[par, gv, :, H_FFN - (FFN_CONV_K - 1):H_FFN, :] = (
                ffn_carry[:, H_FFN - (FFN_CONV_K - 1):H_FFN, c0:c0 + FFN_CHUNK])
            ffn_buf[par, gv, :, H_FFN:H_FFN + t, :] = upj.reshape(bb, t, FFN_CHUNK)
            new_ffn = ffn_buf[par, gv, :, t + H_FFN - (FFN_CONV_K - 1):t + H_FFN, :]
            nsf_ref[:, :, c0:c0 + FFN_CHUNK] = new_ffn
            if n_seq > 1:
                ffn_carry[:, H_FFN - (FFN_CONV_K - 1):H_FFN, c0:c0 + FFN_CHUNK] = new_ffn
        for ch in chunks:
            b0, b1, t0, t1, r0, r1 = ch
            conv = []
            for gv in range(2):
                c0 = cols[gv]
                acc = None
                for k in range(FFN_CONV_K):
                    xk = rd(ffn_buf.at[par, gv], ch, H_FFN - (FFN_CONV_K - 1) + k)
                    term = xk * fcw_ref[k:k + 1, c0:c0 + FFN_CHUNK]
                    acc = term if acc is None else acc + term
                conv.append(acc + fcb_ref[:, c0:c0 + FFN_CHUNK])
            gate, val = conv
            act = gate * _sigmoid(gate) * val
            act_buf[r0:r1, j * FFN_CHUNK:(j + 1) * FFN_CHUNK] = act.astype(jnp.bfloat16)

    h2 = h1 + _dot(act_buf[...], wdown_ref[...])

    n3 = _rms(h2, gple_ref).astype(jnp.bfloat16)
    gate = _sigmoid(_dot(n3, wgate_ref[...]))
    proj = _dot(p_ref[...].reshape(m, D_PLE).astype(jnp.bfloat16), wproj_ref[...])
    h3 = h2 + proj * gate
    if final:
        h3 = _rms(h3, gfin_ref)
    ho_ref[...] = h3.reshape(bb, t, D_MODEL)


def _block_sizes(batch, seq):
    if seq >= 256:
        return 1, 256
    assert ROW_CHUNK % seq == 0 and seq % 16 == 0
    bb = min(batch, max(1, 256 // seq))
    assert batch % bb == 0
    return bb, seq


def _layer(h, p, st_pool, st_sconv, st_cconv, st_ffn, w, pos0, final):
    batch, seq, _ = h.shape
    bb, t = _block_sizes(batch, seq)
    n_seq = seq // t
    grid = (batch // bb, n_seq)
    f32 = jnp.float32

    def act_spec(c):
        return pl.BlockSpec((bb, t, c), lambda b, s: (b, s, 0))

    def state_spec(r, c):
        return pl.BlockSpec((bb, r, c), lambda b, s: (b, 0, 0))

    def const_spec(arr):
        return pl.BlockSpec(arr.shape, lambda b, s: (0,) * arr.ndim,
                            pipeline_mode=pl.Buffered(1))

    in_specs = [act_spec(D_MODEL), act_spec(D_PLE),
                state_spec(POOL_STATE, W_POOL), state_spec(SCONV_K - 1, W_SCONV),
                state_spec(CCONV_K - 1, W_CCONV), state_spec(FFN_CONV_K - 1, 2 * D_FF)]
    in_specs += [const_spec(a) for a in w]
    out_specs = [act_spec(D_MODEL),
                 state_spec(POOL_STATE, W_POOL), state_spec(SCONV_K - 1, W_SCONV),
                 state_spec(CCONV_K - 1, W_CCONV), state_spec(FFN_CONV_K - 1, 2 * D_FF)]
    out_shape = [jax.ShapeDtypeStruct((batch, seq, D_MODEL), f32),
                 jax.ShapeDtypeStruct((batch, POOL_STATE, W_POOL), f32),
                 jax.ShapeDtypeStruct((batch, SCONV_K - 1, W_SCONV), f32),
                 jax.ShapeDtypeStruct((batch, CCONV_K - 1, W_CCONV), f32),
                 jax.ShapeDtypeStruct((batch, FFN_CONV_K - 1, 2 * D_FF), f32)]
    scratch = [pltpu.VMEM((bb, H_POOL + t, W_POOL), f32),
               pltpu.VMEM((bb, H_SCONV + t, W_SCONV), f32),
               pltpu.VMEM((bb, H_CCONV + t, W_CCONV), f32),
               pltpu.VMEM((2, 2, bb, H_FFN + t, FFN_CHUNK), f32),
               pltpu.VMEM((bb, H_FFN, 2 * D_FF), f32),
               pltpu.VMEM((bb * t, D_MODEL), jnp.bfloat16),
               pltpu.VMEM((bb * t, D_FF), jnp.bfloat16)]

    kern = functools.partial(_layer_kernel, bb=bb, t=t, n_seq=n_seq, pos0=pos0, final=final)
    return pl.pallas_call(
        kern, grid=grid, in_specs=in_specs, out_specs=out_specs, out_shape=out_shape,
        scratch_shapes=scratch,
        compiler_params=pltpu.CompilerParams(
            dimension_semantics=("arbitrary", "arbitrary"),
            vmem_limit_bytes=56 * 1024 * 1024),
        name="trunk_layer_final" if final else "trunk_layer",
    )(h, p, st_pool, st_sconv, st_cconv, st_ffn, *w)


def _layer_weights(i, g_mix, w_in, w_pool, pool_scale, sconv_w, sconv_b, cconv_w, cconv_b,
                   ln_g, ln_b, w_out, g_ffn, w_up, ffn_conv_w, ffn_conv_b, w_down, g_ple,
                   w_ple_gate, w_ple_proj, g_final):
    bf16 = jnp.bfloat16
    row = lambda v: v.reshape(1, -1)
    wp = jnp.zeros((W_POOL, W_POOL), jnp.float32)
    for g in range(len(POOL_WINDOWS)):
        lo = g * POOL_GROUP
        wp = wp.at[lo:lo + POOL_GROUP, lo:lo + POOL_GROUP].set(w_pool[i, g])
    return (row(g_mix[i]), w_in[i].astype(bf16), wp.astype(bf16), row(pool_scale[i]),
            sconv_w[i], row(sconv_b[i]), cconv_w[i], row(cconv_b[i]),
            row(ln_g[i]), row(ln_b[i]), w_out[i].astype(bf16), row(g_ffn[i]),
            w_up[i].astype(bf16), ffn_conv_w[i], row(ffn_conv_b[i]), w_down[i].astype(bf16),
            row(g_ple[i]), w_ple_gate[i].astype(bf16), w_ple_proj[i].astype(bf16),
            row(g_final))


def kernel(x_prompt, x_sample, p_prompt, p_sample, state_pool, state_sconv, state_cconv, state_ffn, g_mix, w_in, w_pool, pool_scale, sconv_w, sconv_b, cconv_w, cconv_b, ln_g, ln_b, w_out, g_ffn, w_up, ffn_conv_w, ffn_conv_b, w_down, g_ple, w_ple_gate, w_ple_proj, g_final):
    depth = w_in.shape[0]
    batch = x_prompt.shape[0]
    dt = x_prompt.dtype
    z_pool = jnp.zeros((batch, POOL_STATE, W_POOL), dt)
    z_sconv = jnp.zeros((batch, SCONV_K - 1, W_SCONV), dt)
    z_cconv = jnp.zeros((batch, CCONV_K - 1, W_CCONV), dt)
    z_ffn = jnp.zeros((batch, FFN_CONV_K - 1, 2 * D_FF), dt)
    hp, hs = x_prompt, x_sample
    prompt_states, sample_states = [], []
    for i in range(depth):
        w = _layer_weights(i, g_mix, w_in, w_pool, pool_scale, sconv_w, sconv_b, cconv_w,
                           cconv_b, ln_g, ln_b, w_out, g_ffn, w_up, ffn_conv_w, ffn_conv_b,
                           w_down, g_ple, w_ple_gate, w_ple_proj, g_final)
        final = i == depth - 1
        hp, *st = _layer(hp, p_prompt[i], z_pool, z_sconv, z_cconv, z_ffn, w, 0, final)
        prompt_states.append(st)
        hs, *st = _layer(hs, p_sample[i], state_pool[i], state_sconv[i], state_cconv[i],
                         state_ffn[i], w, PAST_LEN, final)
        sample_states.append(st)
    stack = lambda sts, k: jnp.stack([st[k] for st in sts])
    return (hp, hs,
            stack(prompt_states, 0), stack(prompt_states, 1), stack(prompt_states, 2),
            stack(prompt_states, 3),
            stack(sample_states, 0), stack(sample_states, 1), stack(sample_states, 2),
            stack(sample_states, 3))
```

```python
import collections
import functools

import jax
import jax.numpy as jnp
from jax import lax
from jax.experimental import pallas as pl
from jax.experimental.pallas import tpu as pltpu

D_MODEL = 1024
W_POOL = 256
POOL_WINDOWS = (2, 4, 8, 16)
POOL_GROUP = W_POOL // len(POOL_WINDOWS)
POOL_STATE = max(POOL_WINDOWS) - 1
W_SCONV = 384
W_CCONV = 384
SCONV_K = 3
CCONV_K = 31
D_FF = 2816
FFN_CONV_K = 3
D_PLE = 256
PAST_LEN = 2048
EPS = 1e-6
D_IN = W_POOL + 3 * W_SCONV + 2 * W_CCONV

O_B = W_POOL
O_C = O_B + W_SCONV
O_V = O_C + W_SCONV
O_GA = O_V + W_SCONV
O_GG = O_GA + W_CCONV

V7X_LANES = 128
V7X_BF16_ROWS = 16
V7X_MXU_DIM = 256
V7X_VMEM_BYTES = 64 * 1024 * 1024

ROW_PITCH = 2
FFN_CHUNK = V7X_MXU_DIM
N_FFN_CHUNKS = D_FF // FFN_CHUNK
FFN_SLABS = FFN_CHUNK // V7X_LANES
DOWN_GROUPS = ((0, 4), (4, 8), (8, N_FFN_CHUNKS))
ROW_CHUNK = 64

Chunk = collections.namedtuple("Chunk", "r0 r1 pieces")


def _sigmoid(x):
    return 1.0 / (1.0 + jnp.exp(-x))


def _rms(x, g_ref):
    ms = jnp.mean(x * x, axis=-1, keepdims=True)
    return x * lax.rsqrt(ms + EPS) * g_ref[...]


def _dot(a, b):
    return jnp.dot(a, b, preferred_element_type=jnp.float32)


def _cols(j):
    return slice(j * V7X_LANES, (j + 1) * V7X_LANES)


class _History:
    def __init__(self, ref, hist, t):
        self.ref, self.hist, self.t = ref, hist, t

    def _rows(self, b, frame, n):
        first = b * (self.hist + self.t) + self.hist + frame
        return pl.ds(ROW_PITCH * first, n, stride=ROW_PITCH)

    def read(self, slab, b, frame, n):
        return self.ref[slab, self._rows(b, frame, n), :]

    def write(self, slab, b, frame, val):
        self.ref[slab, self._rows(b, frame, val.shape[0]), :] = val

    def window(self, slab, chunk, back):
        parts = [self.read(slab, b, f0 - back, n) for b, f0, n in chunk.pieces]
        return parts[0] if len(parts) == 1 else jnp.concatenate(parts, axis=0)

    def write_block(self, slab, bb, val):
        for b in range(bb):
            self.write(slab, b, 0, val[b * self.t:(b + 1) * self.t, :])

    def tail(self, slab, b):
        return self.read(slab, b, self.t - self.hist, self.hist)


def _layer_kernel(h_ref, p_ref, stp_ref, sts_ref, stc_ref, stf_ref,
                  gmix_ref, win_ref, wpool_ref, pscale_ref, scw_ref, scb_ref,
                  ccw_ref, ccb_ref, lng_ref, lnb_ref, wout_ref, gffn_ref, wup_ref,
                  fcw_ref, fcb_ref, wdown_ref, gple_ref, wgate_ref, wproj_ref, gfin_ref,
                  ho_ref, nsp_ref, nss_ref, nsc_ref, nsf_ref,
                  pool_buf, sconv_buf, cconv_buf, ffn_buf, ffn_carry,
                  norm_buf, res_buf, bg_buf, diff_buf, mix_buf, act_buf,
                  *, bb, t, n_seq, pos0, final):
    m = bb * t
    s = pl.program_id(1)
    f32, bf16 = jnp.float32, jnp.bfloat16

    if bb == 1:
        chunks = [Chunk(r, r + ROW_CHUNK, ((0, r, ROW_CHUNK),)) for r in range(0, t, ROW_CHUNK)]
    else:
        per = ROW_CHUNK // t
        chunks = [Chunk(b * t, (b + per) * t, tuple((b + i, 0, t) for i in range(per)))
                  for b in range(0, bb, per)]

    pool_h = _History(pool_buf, POOL_STATE, t)
    sconv_h = _History(sconv_buf, SCONV_K - 1, t)
    cconv_h = _History(cconv_buf, CCONV_K - 1, t)

    @pl.when(s == 0)
    def _():
        for b in range(bb):
            for j in range(W_POOL // V7X_LANES):
                pool_h.write(j, b, -POOL_STATE, stp_ref[b, :, _cols(j)])
            for j in range(W_SCONV // V7X_LANES):
                sconv_h.write(j, b, -(SCONV_K - 1), sts_ref[b, :, _cols(j)])
            for j in range(W_CCONV // V7X_LANES):
                cconv_h.write(j, b, -(CCONV_K - 1), stc_ref[b, :, _cols(j)])
        ffn_carry[...] = stf_ref[...]

    def rms_to_norm_buf(src_ref, g_ref):
        for ch in chunks:
            norm_buf[ch.r0:ch.r1, :] = _rms(src_ref[ch.r0:ch.r1, :], g_ref).astype(bf16)

    res_buf[...] = h_ref[...].reshape(m, D_MODEL)
    rms_to_norm_buf(res_buf, gmix_ref)
    n1 = norm_buf[...]
    u_c = _dot(n1, win_ref[:, O_GA:D_IN])
    glu = u_c[:, 0:W_CCONV] * _sigmoid(u_c[:, W_CCONV:2 * W_CCONV])
    for j in range(W_CCONV // V7X_LANES):
        cconv_h.write_block(j, bb, glu[:, _cols(j)])
    u_ab = _dot(n1, win_ref[:, 0:O_GA])
    for j in range(W_POOL // V7X_LANES):
        pool_h.write_block(j, bb, u_ab[:, _cols(j)])
    bg_buf[...] = u_ab[:, O_B:O_C]
    cv = u_ab[:, O_C:O_V] * u_ab[:, O_V:O_GA]
    for j in range(W_SCONV // V7X_LANES):
        sconv_h.write_block(j, bb, cv[:, _cols(j)])

    lane = lax.broadcasted_iota(jnp.int32, (1, V7X_LANES), 1)
    low_half = lane < POOL_GROUP

    for ch in chunks:
        rows = ch.r1 - ch.r0
        cc = []
        for j in range(W_CCONV // V7X_LANES):
            acc = None
            for k in range(CCONV_K):
                term = cconv_h.window(j, ch, CCONV_K - 1 - k) * ccw_ref[k:k + 1, _cols(j)]
                acc = term if acc is None else acc + term
            cc.append(acc + ccb_ref[:, _cols(j)])
        mu = jnp.sum(cc[0] + cc[1] + cc[2], axis=-1, keepdims=True) * (1.0 / W_CCONV)
        xc = [c - mu for c in cc]
        var = jnp.sum(xc[0] * xc[0] + xc[1] * xc[1] + xc[2] * xc[2], axis=-1,
                      keepdims=True) * (1.0 / W_CCONV)
        rstd = lax.rsqrt(var + EPS)
        for j in range(W_CCONV // V7X_LANES):
            ln = xc[j] * rstd * lng_ref[:, _cols(j)] + lnb_ref[:, _cols(j)]
            c0 = O_B + W_SCONV + j * V7X_LANES
            mix_buf[ch.r0:ch.r1, c0:c0 + V7X_LANES] = (ln * _sigmoid(ln)).astype(bf16)

        frame = lax.broadcasted_iota(jnp.int32, (rows, V7X_LANES), 0)
        frame = lax.rem(frame, t) if bb > 1 else frame + ch.r0
        avail = pos0 + s * t + frame + 1
        for j, (w_lo, w_hi) in enumerate(((2, 4), (8, 16))):
            x_now = pool_h.window(j, ch, 0)
            acc = x_now
            sum_lo = None
            for k in range(1, w_hi):
                acc = acc + pool_h.window(j, ch, k)
                if k == w_lo - 1:
                    sum_lo = acc
            ssum = jnp.where(low_half, sum_lo, acc)
            cnt = jnp.minimum(avail, jnp.where(low_half, w_lo, w_hi)).astype(f32)
            diff_buf[ch.r0:ch.r1, _cols(j)] = (ssum / cnt - x_now).astype(bf16)

        for j in range(W_SCONV // V7X_LANES):
            acc = None
            for k in range(SCONV_K):
                term = sconv_h.window(j, ch, SCONV_K - 1 - k) * scw_ref[k:k + 1, _cols(j)]
                acc = term if acc is None else acc + term
            y = bg_buf[ch.r0:ch.r1, _cols(j)] * (acc + scb_ref[:, _cols(j)])
            c0 = O_B + j * V7X_LANES
            mix_buf[ch.r0:ch.r1, c0:c0 + V7X_LANES] = y.astype(bf16)

    for hist, out_ref, width in ((pool_h, nsp_ref, W_POOL), (sconv_h, nss_ref, W_SCONV),
                                 (cconv_h, nsc_ref, W_CCONV)):
        for b in range(bb):
            for j in range(width // V7X_LANES):
                tail = hist.tail(j, b)
                out_ref[b, :, _cols(j)] = tail
                if n_seq > 1:
                    hist.write(j, b, -hist.hist, tail)

    mix_buf[:, 0:O_B] = (_dot(diff_buf[...], wpool_ref[...]) * pscale_ref[...]).astype(bf16)
    res_buf[...] = res_buf[...] + _dot(mix_buf[...], wout_ref[...])

    rms_to_norm_buf(res_buf, gffn_ref)
    n2 = norm_buf[...]
    for j in range(N_FFN_CHUNKS):
        par = j % 2
        cols = (j * FFN_CHUNK, D_FF + j * FFN_CHUNK)
        hists = []
        for gv in range(2):
            c0 = cols[gv]
            up = _dot(n2, wup_ref[:, c0:c0 + FFN_CHUNK])
            hist = _History(ffn_buf.at[par, gv], FFN_CONV_K - 1, t)
            hists.append(hist)
            for q in range(FFN_SLABS):
                cq = slice(c0 + q * V7X_LANES, c0 + (q + 1) * V7X_LANES)
                hist.write_block(q, bb, up[:, _cols(q)])
                for b in range(bb):
                    hist.write(q, b, -(FFN_CONV_K - 1), ffn_carry[b, :, cq])
                    tail = hist.tail(q, b)
                    nsf_ref[b, :, cq] = tail
                    if n_seq > 1:
                        ffn_carry[b, :, cq] = tail
        for ch in chunks:
            for q in range(FFN_SLABS):
                conv = []
                for gv in range(2):
                    cq = slice(cols[gv] + q * V7X_LANES, cols[gv] + (q + 1) * V7X_LANES)
                    acc = None
                    for k in range(FFN_CONV_K):
                        term = hists[gv].window(q, ch, FFN_CONV_K - 1 - k) * fcw_ref[k:k + 1, cq]
                        acc = term if acc is None else acc + term
                    conv.append(acc + fcb_ref[:, cq])
                gate, val = conv
                c0 = j * FFN_CHUNK + q * V7X_LANES
                act_buf[ch.r0:ch.r1, c0:c0 + V7X_LANES] = (gate * _sigmoid(gate) * val).astype(bf16)
        for lo, hi in DOWN_GROUPS:
            if j == hi - 1:
                k0, k1 = lo * FFN_CHUNK, hi * FFN_CHUNK
                res_buf[...] = res_buf[...] + _dot(act_buf[:, k0:k1], wdown_ref[k0:k1, :])

    rms_to_norm_buf(res_buf, gple_ref)
    gate = _sigmoid(_dot(norm_buf[...], wgate_ref[...]))
    proj = _dot(p_ref[...].reshape(m, D_PLE).astype(bf16), wproj_ref[...])
    h3 = res_buf[...] + proj * gate
    if final:
        h3 = _rms(h3, gfin_ref)
    ho_ref[...] = h3.reshape(bb, t, D_MODEL)


def _block_sizes(batch, seq):
    rows = 256
    if seq >= rows:
        assert seq % rows == 0
        return 1, rows
    assert ROW_CHUNK % seq == 0 and seq % V7X_BF16_ROWS == 0
    bb = min(batch, rows // seq)
    assert batch % bb == 0 and (bb * seq) % ROW_CHUNK == 0
    return bb, seq


def _layer(h, p, st_pool, st_sconv, st_cconv, st_ffn, w, pos0, final):
    batch, seq, _ = h.shape
    bb, t = _block_sizes(batch, seq)
    m = bb * t
    n_seq = seq // t
    grid = (batch // bb, n_seq)
    f32, bf16 = jnp.float32, jnp.bfloat16

    def act_spec(c):
        return pl.BlockSpec((bb, t, c), lambda b, s: (b, s, 0))

    def state_spec(r, c):
        return pl.BlockSpec((bb, r, c), lambda b, s: (b, 0, 0))

    def const_spec(arr):
        return pl.BlockSpec(arr.shape, lambda b, s: (0,) * arr.ndim,
                            pipeline_mode=pl.Buffered(1))

    def hist_rows(k):
        return ROW_PITCH * bb * (k - 1 + t)

    state_dims = ((POOL_STATE, W_POOL), (SCONV_K - 1, W_SCONV), (CCONV_K - 1, W_CCONV),
                  (FFN_CONV_K - 1, 2 * D_FF))
    in_specs = [act_spec(D_MODEL), act_spec(D_PLE)] + [state_spec(r, c) for r, c in state_dims]
    in_specs += [const_spec(a) for a in w]
    out_specs = [act_spec(D_MODEL)] + [state_spec(r, c) for r, c in state_dims]
    out_shape = [jax.ShapeDtypeStruct((batch, seq, D_MODEL), f32)]
    out_shape += [jax.ShapeDtypeStruct((batch, r, c), f32) for r, c in state_dims]
    scratch = [pltpu.VMEM((W_POOL // V7X_LANES, hist_rows(POOL_STATE + 1), V7X_LANES), f32),
               pltpu.VMEM((W_SCONV // V7X_LANES, hist_rows(SCONV_K), V7X_LANES), f32),
               pltpu.VMEM((W_CCONV // V7X_LANES, hist_rows(CCONV_K), V7X_LANES), f32),
               pltpu.VMEM((2, 2, FFN_SLABS, hist_rows(FFN_CONV_K), V7X_LANES), f32),
               pltpu.VMEM((bb, FFN_CONV_K - 1, 2 * D_FF), f32),
               pltpu.VMEM((m, D_MODEL), bf16),
               pltpu.VMEM((m, D_MODEL), f32),
               pltpu.VMEM((m, W_SCONV), f32),
               pltpu.VMEM((m, W_POOL), bf16),
               pltpu.VMEM((m, D_MODEL), bf16),
               pltpu.VMEM((m, D_FF), bf16)]

    kern = functools.partial(_layer_kernel, bb=bb, t=t, n_seq=n_seq, pos0=pos0, final=final)
    return pl.pallas_call(
        kern, grid=grid, in_specs=in_specs, out_specs=out_specs, out_shape=out_shape,
        scratch_shapes=scratch,
        compiler_params=pltpu.CompilerParams(
            dimension_semantics=("arbitrary", "arbitrary"),
            vmem_limit_bytes=V7X_VMEM_BYTES - 8 * 1024 * 1024),
        name="trunk_layer_final" if final else "trunk_layer",
    )(h, p, st_pool, st_sconv, st_cconv, st_ffn, *w)


def _layer_weights(i, g_mix, w_in, w_pool, pool_scale, sconv_w, sconv_b, cconv_w, cconv_b,
                   ln_g, ln_b, w_out, g_ffn, w_up, ffn_conv_w, ffn_conv_b, w_down, g_ple,
                   w_ple_gate, w_ple_proj, g_final):
    bf16 = jnp.bfloat16
    row = lambda v: v.reshape(1, -1)
    wp = jnp.zeros((W_POOL, W_POOL), jnp.float32)
    for g in range(len(POOL_WINDOWS)):
        lo = g * POOL_GROUP
        wp = wp.at[lo:lo + POOL_GROUP, lo:lo + POOL_GROUP].set(w_pool[i, g])
    return (row(g_mix[i]), w_in[i].astype(bf16), wp.astype(bf16), row(pool_scale[i]),
            sconv_w[i], row(sconv_b[i]), cconv_w[i], row(cconv_b[i]),
            row(ln_g[i]), row(ln_b[i]), w_out[i].astype(bf16), row(g_ffn[i]),
            w_up[i].astype(bf16), ffn_conv_w[i], row(ffn_conv_b[i]), w_down[i].astype(bf16),
            row(g_ple[i]), w_ple_gate[i].astype(bf16), w_ple_proj[i].astype(bf16),
            row(g_final))


def kernel(x_prompt, x_sample, p_prompt, p_sample, state_pool, state_sconv, state_cconv, state_ffn, g_mix, w_in, w_pool, pool_scale, sconv_w, sconv_b, cconv_w, cconv_b, ln_g, ln_b, w_out, g_ffn, w_up, ffn_conv_w, ffn_conv_b, w_down, g_ple, w_ple_gate, w_ple_proj, g_final):
    depth = w_in.shape[0]
    batch = x_prompt.shape[0]
    dt = x_prompt.dtype
    z_pool = jnp.zeros((batch, POOL_STATE, W_POOL), dt)
    z_sconv = jnp.zeros((batch, SCONV_K - 1, W_SCONV), dt)
    z_cconv = jnp.zeros((batch, CCONV_K - 1, W_CCONV), dt)
    z_ffn = jnp.zeros((batch, FFN_CONV_K - 1, 2 * D_FF), dt)
    hp, hs = x_prompt, x_sample
    prompt_states, sample_states = [], []
    for i in range(depth):
        w = _layer_weights(i, g_mix, w_in, w_pool, pool_scale, sconv_w, sconv_b, cconv_w,
                           cconv_b, ln_g, ln_b, w_out, g_ffn, w_up, ffn_conv_w, ffn_conv_b,
                           w_down, g_ple, w_ple_gate, w_ple_proj, g_final)
        final = i == depth - 1
        hp, *st = _layer(hp, p_prompt[i], z_pool, z_sconv, z_cconv, z_ffn, w, 0, final)
        prompt_states.append(st)
        hs, *st = _layer(hs, p_sample[i], state_pool[i], state_sconv[i], state_cconv[i],
                         state_ffn[i], w, PAST_LEN, final)
        sample_states.append(st)
    stack = lambda sts, k: jnp.stack([st[k] for st in sts])
    return (hp, hs,
            stack(prompt_states, 0), stack(prompt_states, 1), stack(prompt_states, 2),
            stack(prompt_states, 3),
            stack(sample_states, 0), stack(sample_states, 1), stack(sample_states, 2),
            stack(sample_states, 3))
```

```python
import collections
import functools

import jax
import jax.numpy as jnp
from jax import lax
from jax.experimental import pallas as pl
from jax.experimental.pallas import tpu as pltpu

D_MODEL = 1024
W_POOL = 256
POOL_WINDOWS = (2, 4, 8, 16)
POOL_GROUP = W_POOL // len(POOL_WINDOWS)
POOL_STATE = max(POOL_WINDOWS) - 1
W_SCONV = 384
W_CCONV = 384
SCONV_K = 3
CCONV_K = 31
D_FF = 2816
FFN_CONV_K = 3
D_PLE = 256
PAST_LEN = 2048
EPS = 1e-6
D_IN = W_POOL + 3 * W_SCONV + 2 * W_CCONV

O_B = W_POOL
O_C = O_B + W_SCONV
O_V = O_C + W_SCONV
O_GA = O_V + W_SCONV
O_GG = O_GA + W_CCONV

V7X_LANES = 128
V7X_BF16_ROWS = 16
V7X_MXU_DIM = 256
V7X_VMEM_BYTES = 64 * 1024 * 1024

ROW_PITCH = 2
FFN_CHUNK = V7X_MXU_DIM
N_FFN_CHUNKS = D_FF // FFN_CHUNK
FFN_SLABS = FFN_CHUNK // V7X_LANES
DOWN_GROUP = 2
ROW_CHUNK = 64
BLOCK_ROWS = 256

Chunk = collections.namedtuple("Chunk", "r0 r1 pieces")


def _sigmoid(x):
    return 1.0 / (1.0 + jnp.exp(-x))


def _rms(x, g_ref):
    ms = jnp.mean(x * x, axis=-1, keepdims=True)
    return x * lax.rsqrt(ms + EPS) * g_ref[...]


def _dot(a, b):
    return jnp.dot(a, b, preferred_element_type=jnp.float32)


def _cols(j):
    return slice(j * V7X_LANES, (j + 1) * V7X_LANES)


class _History:
    def __init__(self, ref, hist, t):
        self.ref, self.hist, self.t = ref, hist, t

    def _rows(self, b, frame, n):
        first = b * (self.hist + self.t) + self.hist + frame
        return pl.ds(ROW_PITCH * first, n, stride=ROW_PITCH)

    def read(self, slab, b, frame, n):
        return self.ref[slab, self._rows(b, frame, n), :]

    def write(self, slab, b, frame, val):
        self.ref[slab, self._rows(b, frame, val.shape[0]), :] = val

    def window(self, slab, chunk, back):
        parts = [self.read(slab, b, f0 - back, n) for b, f0, n in chunk.pieces]
        return parts[0] if len(parts) == 1 else jnp.concatenate(parts, axis=0)

    def write_block(self, slab, bb, val):
        for b in range(bb):
            self.write(slab, b, 0, val[b * self.t:(b + 1) * self.t, :])

    def tail(self, slab, b):
        return self.read(slab, b, self.t - self.hist, self.hist)


def _layer_kernel(h_ref, p_ref, stp_ref, sts_ref, stc_ref, stf_ref,
                  gmix_ref, win_ref, wpool_ref, pscale_ref, scw_ref, scb_ref,
                  ccw_ref, ccb_ref, lng_ref, lnb_ref, wout_ref, gffn_ref, wup_ref,
                  fcw_ref, fcb_ref, wdown_ref, gple_ref, wgate_ref, wproj_ref, gfin_ref,
                  ho_ref, nsp_ref, nss_ref, nsc_ref, nsf_ref,
                  pool_buf, sconv_buf, cconv_buf, ffn_buf, ffn_carry,
                  norm_buf, res_buf, bg_buf, diff_buf, mix_buf, act_buf,
                  *, bb, t, n_seq, pos0, final):
    m = bb * t
    s = pl.program_id(1)
    f32, bf16 = jnp.float32, jnp.bfloat16

    if bb == 1:
        chunks = [Chunk(r, r + ROW_CHUNK, ((0, r, ROW_CHUNK),)) for r in range(0, t, ROW_CHUNK)]
    else:
        per = ROW_CHUNK // t
        chunks = [Chunk(b * t, (b + per) * t, tuple((b + i, 0, t) for i in range(per)))
                  for b in range(0, bb, per)]

    pool_h = _History(pool_buf, POOL_STATE, t)
    sconv_h = _History(sconv_buf, SCONV_K - 1, t)
    cconv_h = _History(cconv_buf, CCONV_K - 1, t)

    @pl.when(s == 0)
    def _():
        for b in range(bb):
            for j in range(W_POOL // V7X_LANES):
                pool_h.write(j, b, -POOL_STATE, stp_ref[b, :, _cols(j)])
            for j in range(W_SCONV // V7X_LANES):
                sconv_h.write(j, b, -(SCONV_K - 1), sts_ref[b, :, _cols(j)])
            for j in range(W_CCONV // V7X_LANES):
                cconv_h.write(j, b, -(CCONV_K - 1), stc_ref[b, :, _cols(j)])
        ffn_carry[...] = stf_ref[...]

    def rms_to_norm_buf(src_ref, g_ref):
        for ch in chunks:
            norm_buf[ch.r0:ch.r1, :] = _rms(src_ref[ch.r0:ch.r1, :], g_ref).astype(bf16)

    res_buf[...] = h_ref[...].reshape(m, D_MODEL)
    rms_to_norm_buf(res_buf, gmix_ref)
    n1 = norm_buf[...]
    u_c = _dot(n1, win_ref[:, O_GA:D_IN])
    glu = u_c[:, 0:W_CCONV] * _sigmoid(u_c[:, W_CCONV:2 * W_CCONV])
    for j in range(W_CCONV // V7X_LANES):
        cconv_h.write_block(j, bb, glu[:, _cols(j)])
    u_ab = _dot(n1, win_ref[:, 0:O_GA])
    for j in range(W_POOL // V7X_LANES):
        pool_h.write_block(j, bb, u_ab[:, _cols(j)])
    bg_buf[...] = u_ab[:, O_B:O_C]
    cv = u_ab[:, O_C:O_V] * u_ab[:, O_V:O_GA]
    for j in range(W_SCONV // V7X_LANES):
        sconv_h.write_block(j, bb, cv[:, _cols(j)])

    lane = lax.broadcasted_iota(jnp.int32, (1, V7X_LANES), 1)
    low_half = lane < POOL_GROUP

    for ch in chunks:
        rows = ch.r1 - ch.r0
        cc = []
        for j in range(W_CCONV // V7X_LANES):
            acc = None
            for k in range(CCONV_K):
                term = cconv_h.window(j, ch, CCONV_K - 1 - k) * ccw_ref[k:k + 1, _cols(j)]
                acc = term if acc is None else acc + term
            cc.append(acc + ccb_ref[:, _cols(j)])
        mu = jnp.sum(cc[0] + cc[1] + cc[2], axis=-1, keepdims=True) * (1.0 / W_CCONV)
        xc = [c - mu for c in cc]
        var = jnp.sum(xc[0] * xc[0] + xc[1] * xc[1] + xc[2] * xc[2], axis=-1,
                      keepdims=True) * (1.0 / W_CCONV)
        rstd = lax.rsqrt(var + EPS)
        for j in range(W_CCONV // V7X_LANES):
            ln = xc[j] * rstd * lng_ref[:, _cols(j)] + lnb_ref[:, _cols(j)]
            c0 = O_B + W_SCONV + j * V7X_LANES
            mix_buf[ch.r0:ch.r1, c0:c0 + V7X_LANES] = (ln * _sigmoid(ln)).astype(bf16)

        frame = lax.broadcasted_iota(jnp.int32, (rows, V7X_LANES), 0)
        frame = lax.rem(frame, t) if bb > 1 else frame + ch.r0
        avail = pos0 + s * t + frame + 1
        for j, (w_lo, w_hi) in enumerate(((2, 4), (8, 16))):
            x_now = pool_h.window(j, ch, 0)
            acc = x_now
            sum_lo = None
            for k in range(1, w_hi):
                acc = acc + pool_h.window(j, ch, k)
                if k == w_lo - 1:
                    sum_lo = acc
            ssum = jnp.where(low_half, sum_lo, acc)
            cnt = jnp.minimum(avail, jnp.where(low_half, w_lo, w_hi)).astype(f32)
            diff_buf[ch.r0:ch.r1, _cols(j)] = (ssum / cnt - x_now).astype(bf16)

        for j in range(W_SCONV // V7X_LANES):
            acc = None
            for k in range(SCONV_K):
                term = sconv_h.window(j, ch, SCONV_K - 1 - k) * scw_ref[k:k + 1, _cols(j)]
                acc = term if acc is None else acc + term
            y = bg_buf[ch.r0:ch.r1, _cols(j)] * (acc + scb_ref[:, _cols(j)])
            c0 = O_B + j * V7X_LANES
            mix_buf[ch.r0:ch.r1, c0:c0 + V7X_LANES] = y.astype(bf16)

    for hist, out_ref, width in ((pool_h, nsp_ref, W_POOL), (sconv_h, nss_ref, W_SCONV),
                                 (cconv_h, nsc_ref, W_CCONV)):
        for b in range(bb):
            for j in range(width // V7X_LANES):
                tail = hist.tail(j, b)
                out_ref[b, :, _cols(j)] = tail
                if n_seq > 1:
                    hist.write(j, b, -hist.hist, tail)

    mix_buf[:, 0:O_B] = (_dot(diff_buf[...], wpool_ref[...]) * pscale_ref[...]).astype(bf16)
    res_buf[...] = res_buf[...] + _dot(mix_buf[...], wout_ref[...])

    rms_to_norm_buf(res_buf, gffn_ref)
    n2 = norm_buf[...]

    def up_chunk(j):
        hists = []
        for gv, c0 in enumerate((j * FFN_CHUNK, D_FF + j * FFN_CHUNK)):
            up = _dot(n2, wup_ref[:, c0:c0 + FFN_CHUNK])
            hist = _History(ffn_buf.at[j % 2, gv], FFN_CONV_K - 1, t)
            hists.append(hist)
            for q in range(FFN_SLABS):
                cq = slice(c0 + q * V7X_LANES, c0 + (q + 1) * V7X_LANES)
                hist.write_block(q, bb, up[:, _cols(q)])
                for b in range(bb):
                    hist.write(q, b, -hist.hist, ffn_carry[b, :, cq])
                    tail = hist.tail(q, b)
                    nsf_ref[b, :, cq] = tail
                    if n_seq > 1:
                        ffn_carry[b, :, cq] = tail
        return hists

    def conv_chunk(j, hists):
        for ch in chunks:
            for q in range(FFN_SLABS):
                conv = []
                for gv, c0 in enumerate((j * FFN_CHUNK, D_FF + j * FFN_CHUNK)):
                    cq = slice(c0 + q * V7X_LANES, c0 + (q + 1) * V7X_LANES)
                    acc = None
                    for k in range(FFN_CONV_K):
                        term = hists[gv].window(q, ch, FFN_CONV_K - 1 - k) * fcw_ref[k:k + 1, cq]
                        acc = term if acc is None else acc + term
                    conv.append(acc + fcb_ref[:, cq])
                gate, val = conv
                c0 = j * FFN_CHUNK + q * V7X_LANES
                act_buf[ch.r0:ch.r1, c0:c0 + V7X_LANES] = (
                    gate * _sigmoid(gate) * val).astype(bf16)

    pending = up_chunk(0)
    for j in range(N_FFN_CHUNKS):
        following = up_chunk(j + 1) if j + 1 < N_FFN_CHUNKS else None
        conv_chunk(j, pending)
        pending = following
        if (j + 1) % DOWN_GROUP == 0 or j + 1 == N_FFN_CHUNKS:
            k0, k1 = (j // DOWN_GROUP) * DOWN_GROUP * FFN_CHUNK, (j + 1) * FFN_CHUNK
            res_buf[...] = res_buf[...] + _dot(act_buf[:, k0:k1], wdown_ref[k0:k1, :])

    rms_to_norm_buf(res_buf, gple_ref)
    gate = _sigmoid(_dot(norm_buf[...], wgate_ref[...]))
    proj = _dot(p_ref[...].reshape(m, D_PLE).astype(bf16), wproj_ref[...])
    h3 = res_buf[...] + proj * gate
    if final:
        h3 = _rms(h3, gfin_ref)
    ho_ref[...] = h3.reshape(bb, t, D_MODEL)


def _block_sizes(batch, seq):
    if seq >= BLOCK_ROWS:
        assert seq % BLOCK_ROWS == 0
        return 1, BLOCK_ROWS
    assert ROW_CHUNK % seq == 0 and seq % V7X_BF16_ROWS == 0
    bb = min(batch, BLOCK_ROWS // seq)
    assert batch % bb == 0 and (bb * seq) % ROW_CHUNK == 0
    return bb, seq


def _layer(h, p, states, weights, layer, state_layer, pos0, final):
    batch, seq, _ = h.shape
    bb, t = _block_sizes(batch, seq)
    m = bb * t
    n_seq = seq // t
    grid = (batch // bb, n_seq)
    f32, bf16 = jnp.float32, jnp.bfloat16

    def act_spec(c):
        return pl.BlockSpec((bb, t, c), lambda b, s: (b, s, 0))

    def state_spec(r, c):
        return pl.BlockSpec((bb, r, c), lambda b, s: (b, 0, 0))

    def stacked_state_spec(r, c):
        return pl.BlockSpec((None, bb, r, c), lambda b, s: (state_layer, b, 0, 0))

    def weight_spec(arr):
        idx = layer if arr.shape[0] > 1 else 0
        return pl.BlockSpec((None,) + arr.shape[1:], lambda b, s: (idx,) + (0,) * (arr.ndim - 1),
                            pipeline_mode=pl.Buffered(1))

    def hist_rows(k):
        return ROW_PITCH * bb * (k - 1 + t)

    state_dims = ((POOL_STATE, W_POOL), (SCONV_K - 1, W_SCONV), (CCONV_K - 1, W_CCONV),
                  (FFN_CONV_K - 1, 2 * D_FF))
    in_specs = [act_spec(D_MODEL),
                pl.BlockSpec((None, bb, t, D_PLE), lambda b, s: (layer, b, s, 0))]
    in_specs += [stacked_state_spec(r, c) for r, c in state_dims]
    in_specs += [weight_spec(a) for a in weights]
    out_specs = [act_spec(D_MODEL)] + [state_spec(r, c) for r, c in state_dims]
    out_shape = [jax.ShapeDtypeStruct((batch, seq, D_MODEL), f32)]
    out_shape += [jax.ShapeDtypeStruct((batch, r, c), f32) for r, c in state_dims]
    scratch = [pltpu.VMEM((W_POOL // V7X_LANES, hist_rows(POOL_STATE + 1), V7X_LANES), f32),
               pltpu.VMEM((W_SCONV // V7X_LANES, hist_rows(SCONV_K), V7X_LANES), f32),
               pltpu.VMEM((W_CCONV // V7X_LANES, hist_rows(CCONV_K), V7X_LANES), f32),
               pltpu.VMEM((2, 2, FFN_SLABS, hist_rows(FFN_CONV_K), V7X_LANES), f32),
               pltpu.VMEM((bb, FFN_CONV_K - 1, 2 * D_FF), f32),
               pltpu.VMEM((m, D_MODEL), bf16),
               pltpu.VMEM((m, D_MODEL), f32),
               pltpu.VMEM((m, W_SCONV), f32),
               pltpu.VMEM((m, W_POOL), bf16),
               pltpu.VMEM((m, D_MODEL), bf16),
               pltpu.VMEM((m, D_FF), bf16)]

    kern = functools.partial(_layer_kernel, bb=bb, t=t, n_seq=n_seq, pos0=pos0, final=final)
    return pl.pallas_call(
        kern, grid=grid, in_specs=in_specs, out_specs=out_specs, out_shape=out_shape,
        scratch_shapes=scratch,
        compiler_params=pltpu.CompilerParams(
            dimension_semantics=("arbitrary", "arbitrary"),
            vmem_limit_bytes=V7X_VMEM_BYTES - 8 * 1024 * 1024),
        name="trunk_layer_final" if final else "trunk_layer",
    )(h, p, *states, *weights)


def _stacked_weights(g_mix, w_in, w_pool, pool_scale, sconv_w, sconv_b, cconv_w, cconv_b,
                     ln_g, ln_b, w_out, g_ffn, w_up, ffn_conv_w, ffn_conv_b, w_down, g_ple,
                     w_ple_gate, w_ple_proj, g_final):
    bf16 = jnp.bfloat16
    rows = lambda v: v[:, None, :]
    n_groups = len(POOL_WINDOWS)
    eye = jnp.eye(n_groups, dtype=w_pool.dtype)
    wp = jnp.einsum("lgcd,gh->lgchd", w_pool, eye).reshape(-1, W_POOL, W_POOL)
    return (rows(g_mix), w_in.astype(bf16), wp.astype(bf16), rows(pool_scale),
            sconv_w, rows(sconv_b), cconv_w, rows(cconv_b), rows(ln_g), rows(ln_b),
            w_out.astype(bf16), rows(g_ffn), w_up.astype(bf16), ffn_conv_w, rows(ffn_conv_b),
            w_down.astype(bf16), rows(g_ple), w_ple_gate.astype(bf16), w_ple_proj.astype(bf16),
            g_final[None, None, :])


def kernel(x_prompt, x_sample, p_prompt, p_sample, state_pool, state_sconv, state_cconv, state_ffn, g_mix, w_in, w_pool, pool_scale, sconv_w, sconv_b, cconv_w, cconv_b, ln_g, ln_b, w_out, g_ffn, w_up, ffn_conv_w, ffn_conv_b, w_down, g_ple, w_ple_gate, w_ple_proj, g_final):
    depth = w_in.shape[0]
    batch = x_prompt.shape[0]
    dt = x_prompt.dtype
    zero_states = (jnp.zeros((1, batch, POOL_STATE, W_POOL), dt),
                   jnp.zeros((1, batch, SCONV_K - 1, W_SCONV), dt),
                   jnp.zeros((1, batch, CCONV_K - 1, W_CCONV), dt),
                   jnp.zeros((1, batch, FFN_CONV_K - 1, 2 * D_FF), dt))
    sample_states_in = (state_pool, state_sconv, state_cconv, state_ffn)
    weights = _stacked_weights(g_mix, w_in, w_pool, pool_scale, sconv_w, sconv_b, cconv_w,
                               cconv_b, ln_g, ln_b, w_out, g_ffn, w_up, ffn_conv_w, ffn_conv_b,
                               w_down, g_ple, w_ple_gate, w_ple_proj, g_final)
    hp, hs = x_prompt, x_sample
    prompt_states, sample_states = [], []
    for i in range(depth):
        final = i == depth - 1
        hp, *st = _layer(hp, p_prompt, zero_states, weights, i, 0, 0, final)
        prompt_states.append(st)
        hs, *st = _layer(hs, p_sample, sample_states_in, weights, i, i, PAST_LEN, final)
        sample_states.append(st)
    stack = lambda sts, k: jnp.stack([st[k] for st in sts])
    return (hp, hs,
            stack(prompt_states, 0), stack(prompt_states, 1), stack(prompt_states, 2),
            stack(prompt_states, 3),
            stack(sample_states, 0), stack(sample_states, 1), stack(sample_states, 2),
            stack(sample_states, 3))
```

```python
import collections
import functools

import jax
import jax.numpy as jnp
from jax import lax
from jax.experimental import pallas as pl
from jax.experimental.pallas import tpu as pltpu

D_MODEL = 1024
W_POOL = 256
POOL_WINDOWS = (2, 4, 8, 16)
POOL_GROUP = W_POOL // len(POOL_WINDOWS)
POOL_STATE = max(POOL_WINDOWS) - 1
W_SCONV = 384
W_CCONV = 384
SCONV_K = 3
CCONV_K = 31
D_FF = 2816
FFN_CONV_K = 3
D_PLE = 256
PAST_LEN = 2048
EPS = 1e-6
D_IN = W_POOL + 3 * W_SCONV + 2 * W_CCONV

O_B = W_POOL
O_C = O_B + W_SCONV
O_V = O_C + W_SCONV
O_GA = O_V + W_SCONV
O_GG = O_GA + W_CCONV

V7X_LANES = 128
V7X_BF16_ROWS = 16
V7X_MXU_DIM = 256
V7X_VMEM_BYTES = 64 * 1024 * 1024

ROW_PITCH = 2
FFN_CHUNK = V7X_MXU_DIM
N_FFN_CHUNKS = D_FF // FFN_CHUNK
FFN_SLABS = FFN_CHUNK // V7X_LANES
DOWN_GROUP = 1
ROW_CHUNK = 32
SUB_ROWS = 256
BLOCK_ROWS = 512

Chunk = collections.namedtuple("Chunk", "r0 r1 pieces")


def _sigmoid(x):
    return 1.0 / (1.0 + jnp.exp(-x))


def _rms(x, g_ref):
    ms = jnp.mean(x * x, axis=-1, keepdims=True)
    return x * lax.rsqrt(ms + EPS) * g_ref[...]


def _dot(a, b):
    return jnp.dot(a, b, preferred_element_type=jnp.float32)


def _cols(j):
    return slice(j * V7X_LANES, (j + 1) * V7X_LANES)


class _History:
    def __init__(self, ref, hist, t):
        self.ref, self.hist, self.t = ref, hist, t

    def _rows(self, b, frame, n):
        first = b * (self.hist + self.t) + self.hist + frame
        return pl.ds(ROW_PITCH * first, n, stride=ROW_PITCH)

    def read(self, slab, b, frame, n):
        return self.ref[slab, self._rows(b, frame, n), :]

    def write(self, slab, b, frame, val):
        self.ref[slab, self._rows(b, frame, val.shape[0]), :] = val

    def window(self, slab, chunk, back):
        parts = [self.read(slab, b, f0 - back, n) for b, f0, n in chunk.pieces]
        return parts[0] if len(parts) == 1 else jnp.concatenate(parts, axis=0)

    def write_rows(self, slab, chunk, val):
        off = 0
        for b, f0, n in chunk.pieces:
            self.write(slab, b, f0, val[off:off + n, :])
            off += n

    def tail(self, slab, b):
        return self.read(slab, b, self.t - self.hist, self.hist)


def _layer_kernel(h_ref, p_ref, stp_ref, sts_ref, stc_ref, stf_ref,
                  gmix_ref, win_ref, wpool_ref, pscale_ref, scw_ref, scb_ref,
                  ccw_ref, ccb_ref, lng_ref, lnb_ref, wout_ref, gffn_ref, wup_ref,
                  fcw_ref, fcb_ref, wdown_ref, gple_ref, wgate_ref, wproj_ref, gfin_ref,
                  ho_ref, nsp_ref, nss_ref, nsc_ref, nsf_ref,
                  pool_buf, sconv_buf, cconv_buf, ffn_buf, ffn_carry,
                  norm1_buf, norm2_buf, res_buf, bg_buf, diff_buf, mix_buf, act_buf,
                  *, bb, t, n_seq, pos0, final):
    m = bb * t
    s = pl.program_id(1)
    f32, bf16 = jnp.float32, jnp.bfloat16

    if bb == 1:
        chunks = [Chunk(r, r + ROW_CHUNK, ((0, r, ROW_CHUNK),)) for r in range(0, t, ROW_CHUNK)]
        subs = [Chunk(r, r + SUB_ROWS, ((0, r, SUB_ROWS),)) for r in range(0, t, SUB_ROWS)]
        ffn_frames = SUB_ROWS
    else:
        per = ROW_CHUNK // t
        chunks = [Chunk(b * t, (b + per) * t, tuple((b + i, 0, t) for i in range(per)))
                  for b in range(0, bb, per)]
        subs = [Chunk(0, m, tuple((b, 0, t) for b in range(bb)))]
        ffn_frames = t
    per_sub = len(chunks) // len(subs)
    sub_chunks = [chunks[k * per_sub:(k + 1) * per_sub] for k in range(len(subs))]

    def local(chunk, sub):
        f_sub = sub.pieces[0][1]
        return Chunk(chunk.r0, chunk.r1, tuple((b, f0 - f_sub, n) for b, f0, n in chunk.pieces))

    pool_h = _History(pool_buf, POOL_STATE, t)
    sconv_h = _History(sconv_buf, SCONV_K - 1, t)
    cconv_h = _History(cconv_buf, CCONV_K - 1, t)
    mixers = ((pool_h, stp_ref, nsp_ref, W_POOL), (sconv_h, sts_ref, nss_ref, W_SCONV),
              (cconv_h, stc_ref, nsc_ref, W_CCONV))

    @pl.when(s == 0)
    def _():
        for hist, st_ref, _, width in mixers:
            for b in range(bb):
                for j in range(width // V7X_LANES):
                    hist.write(j, b, -hist.hist, st_ref[b, :, _cols(j)])
        ffn_carry[...] = stf_ref[...]

    lane = lax.broadcasted_iota(jnp.int32, (1, V7X_LANES), 1)
    low_half = lane < POOL_GROUP

    def rows_of(ref3, ch):
        b0, b1 = ch.pieces[0][0], ch.pieces[-1][0] + 1
        f0, f1 = ch.pieces[0][1], ch.pieces[0][1] + ch.pieces[0][2]
        return ref3[b0:b1, f0:f1, :].reshape(ch.r1 - ch.r0, ref3.shape[-1])

    def rms_in(k):
        for ch in sub_chunks[k]:
            norm1_buf[ch.r0:ch.r1, :] = _rms(rows_of(h_ref, ch), gmix_ref).astype(bf16)

    def proj_in(k):
        sub = subs[k]
        n1 = norm1_buf[sub.r0:sub.r1, :]
        u_c = _dot(n1, win_ref[:, O_GA:D_IN])
        glu = u_c[:, 0:W_CCONV] * _sigmoid(u_c[:, W_CCONV:2 * W_CCONV])
        for j in range(W_CCONV // V7X_LANES):
            cconv_h.write_rows(j, sub, glu[:, _cols(j)])
        u_ab = _dot(n1, win_ref[:, 0:O_GA])
        for j in range(W_POOL // V7X_LANES):
            pool_h.write_rows(j, sub, u_ab[:, _cols(j)])
        bg_buf[sub.r0:sub.r1, :] = u_ab[:, O_B:O_C]
        cv = u_ab[:, O_C:O_V] * u_ab[:, O_V:O_GA]
        for j in range(W_SCONV // V7X_LANES):
            sconv_h.write_rows(j, sub, cv[:, _cols(j)])

    def mix_chunk(ch):
        rows = ch.r1 - ch.r0
        cc = []
        for j in range(W_CCONV // V7X_LANES):
            acc = None
            for k in range(CCONV_K):
                term = cconv_h.window(j, ch, CCONV_K - 1 - k) * ccw_ref[k:k + 1, _cols(j)]
                acc = term if acc is None else acc + term
            cc.append(acc + ccb_ref[:, _cols(j)])
        mu = jnp.sum(cc[0] + cc[1] + cc[2], axis=-1, keepdims=True) * (1.0 / W_CCONV)
        xc = [c - mu for c in cc]
        var = jnp.sum(xc[0] * xc[0] + xc[1] * xc[1] + xc[2] * xc[2], axis=-1,
                      keepdims=True) * (1.0 / W_CCONV)
        rstd = lax.rsqrt(var + EPS)
        for j in range(W_CCONV // V7X_LANES):
            ln = xc[j] * rstd * lng_ref[:, _cols(j)] + lnb_ref[:, _cols(j)]
            c0 = O_B + W_SCONV + j * V7X_LANES
            mix_buf[ch.r0:ch.r1, c0:c0 + V7X_LANES] = (ln * _sigmoid(ln)).astype(bf16)

        frame = lax.broadcasted_iota(jnp.int32, (rows, V7X_LANES), 0)
        frame = lax.rem(frame, t) if bb > 1 else frame + ch.r0
        avail = pos0 + s * t + frame + 1
        for j, (w_lo, w_hi) in enumerate(((2, 4), (8, 16))):
            x_now = pool_h.window(j, ch, 0)
            acc = x_now
            sum_lo = None
            for k in range(1, w_hi):
                acc = acc + pool_h.window(j, ch, k)
                if k == w_lo - 1:
                    sum_lo = acc
            ssum = jnp.where(low_half, sum_lo, acc)
            cnt = jnp.minimum(avail, jnp.where(low_half, w_lo, w_hi)).astype(f32)
            diff_buf[ch.r0:ch.r1, _cols(j)] = (ssum / cnt - x_now).astype(bf16)

        for j in range(W_SCONV // V7X_LANES):
            acc = None
            for k in range(SCONV_K):
                term = sconv_h.window(j, ch, SCONV_K - 1 - k) * scw_ref[k:k + 1, _cols(j)]
                acc = term if acc is None else acc + term
            y = bg_buf[ch.r0:ch.r1, _cols(j)] * (acc + scb_ref[:, _cols(j)])
            c0 = O_B + j * V7X_LANES
            mix_buf[ch.r0:ch.r1, c0:c0 + V7X_LANES] = y.astype(bf16)

    def proj_out(k):
        sub = subs[k]
        r = slice(sub.r0, sub.r1)
        mix_buf[r, 0:O_B] = (_dot(diff_buf[r, :], wpool_ref[...]) * pscale_ref[...]).astype(bf16)
        h = h_ref[...].reshape(m, D_MODEL)[r, :] if bb > 1 else h_ref[0, r, :]
        res_buf[r, :] = h + _dot(mix_buf[r, :], wout_ref[...])

    def rms_res(k, g_ref):
        for ch in sub_chunks[k]:
            norm2_buf[ch.r0:ch.r1, :] = _rms(res_buf[ch.r0:ch.r1, :], g_ref).astype(bf16)

    def up_chunk(k, j):
        sub = subs[k]
        n2 = norm2_buf[sub.r0:sub.r1, :]
        hists = []
        for gv, c0 in enumerate((j * FFN_CHUNK, D_FF + j * FFN_CHUNK)):
            up = _dot(n2, wup_ref[:, c0:c0 + FFN_CHUNK])
            hist = _History(ffn_buf.at[j % 2, gv], FFN_CONV_K - 1, ffn_frames)
            hists.append(hist)
            for q in range(FFN_SLABS):
                cq = slice(c0 + q * V7X_LANES, c0 + (q + 1) * V7X_LANES)
                hist.write_rows(q, local(sub, sub), up[:, _cols(q)])
                for b in range(bb):
                    hist.write(q, b, -hist.hist, ffn_carry[b, :, cq])
                    tail = hist.tail(q, b)
                    nsf_ref[b, :, cq] = tail
                    ffn_carry[b, :, cq] = tail
        return hists

    def conv_chunk(k, j, hists):
        for ch in sub_chunks[k]:
            lch = local(ch, subs[k])
            for q in range(FFN_SLABS):
                conv = []
                for gv, c0 in enumerate((j * FFN_CHUNK, D_FF + j * FFN_CHUNK)):
                    cq = slice(c0 + q * V7X_LANES, c0 + (q + 1) * V7X_LANES)
                    acc = None
                    for kk in range(FFN_CONV_K):
                        term = (hists[gv].window(q, lch, FFN_CONV_K - 1 - kk)
                                * fcw_ref[kk:kk + 1, cq])
                        acc = term if acc is None else acc + term
                    conv.append(acc + fcb_ref[:, cq])
                gate, val = conv
                c0 = j * FFN_CHUNK + q * V7X_LANES
                act_buf[ch.r0:ch.r1, c0:c0 + V7X_LANES] = (
                    gate * _sigmoid(gate) * val).astype(bf16)

    def down_after(k, j):
        if (j + 1) % DOWN_GROUP == 0 or j + 1 == N_FFN_CHUNKS:
            r = slice(subs[k].r0, subs[k].r1)
            k0, k1 = (j // DOWN_GROUP) * DOWN_GROUP * FFN_CHUNK, (j + 1) * FFN_CHUNK
            res_buf[r, :] = res_buf[r, :] + _dot(act_buf[r, k0:k1], wdown_ref[k0:k1, :])

    def ffn(k, side_work):
        side = list(side_work)
        pending = up_chunk(k, 0)
        for j in range(N_FFN_CHUNKS):
            following = up_chunk(k, j + 1) if j + 1 < N_FFN_CHUNKS else None
            conv_chunk(k, j, pending)
            pending = following
            if side:
                side.pop(0)()
            down_after(k, j)
        for unit in side:
            unit()

    def embed(k):
        r = slice(subs[k].r0, subs[k].r1)
        gate = _sigmoid(_dot(norm2_buf[r, :], wgate_ref[...]))
        p = p_ref[...].reshape(m, D_PLE)[r, :] if bb > 1 else p_ref[0, r, :]
        proj = _dot(p.astype(bf16), wproj_ref[...])
        h3 = res_buf[r, :] + proj * gate
        if final:
            h3 = _rms(h3, gfin_ref)
        if bb > 1:
            ho_ref[...] = h3.reshape(bb, t, D_MODEL)
        else:
            ho_ref[0, r, :] = h3

    def carry_mixer_history():
        for hist, _, out_ref, width in mixers:
            for b in range(bb):
                for j in range(width // V7X_LANES):
                    tail = hist.tail(j, b)
                    out_ref[b, :, _cols(j)] = tail
                    hist.write(j, b, -hist.hist, tail)

    if len(subs) == 1:
        rms_in(0)
        proj_in(0)
        for ch in sub_chunks[0]:
            mix_chunk(ch)
        carry_mixer_history()
        proj_out(0)
        rms_res(0, gffn_ref)
        ffn(0, ())
        rms_res(0, gple_ref)
        embed(0)
    else:
        assert len(subs) == 2
        rms_in(0)
        proj_in(0)
        rms_in(1)
        proj_in(1)
        for ch in sub_chunks[0]:
            mix_chunk(ch)
        proj_out(0)
        rms_res(0, gffn_ref)
        side = [functools.partial(mix_chunk, ch) for ch in sub_chunks[1]]
        side += [carry_mixer_history, functools.partial(proj_out, 1)]
        ffn(0, side)
        rms_res(0, gple_ref)
        rms_res(1, gffn_ref)
        ffn(1, [functools.partial(embed, 0)])
        rms_res(1, gple_ref)
        embed(1)


def _block_sizes(batch, seq):
    if seq >= BLOCK_ROWS:
        assert seq % BLOCK_ROWS == 0
        return 1, BLOCK_ROWS
    assert ROW_CHUNK % seq == 0 and seq % V7X_BF16_ROWS == 0
    bb = min(batch, SUB_ROWS // seq)
    assert batch % bb == 0 and (bb * seq) % ROW_CHUNK == 0
    return bb, seq


def _layer(h, p, states, weights, layer, state_layer, pos0, final):
    batch, seq, _ = h.shape
    bb, t = _block_sizes(batch, seq)
    m = bb * t
    n_seq = seq // t
    grid = (batch // bb, n_seq)
    f32, bf16 = jnp.float32, jnp.bfloat16

    def act_spec(c):
        return pl.BlockSpec((bb, t, c), lambda b, s: (b, s, 0))

    def state_spec(r, c):
        return pl.BlockSpec((bb, r, c), lambda b, s: (b, 0, 0))

    def stacked_state_spec(r, c):
        return pl.BlockSpec((None, bb, r, c), lambda b, s: (state_layer, b, 0, 0))

    def weight_spec(arr):
        idx = layer if arr.shape[0] > 1 else 0
        return pl.BlockSpec((None,) + arr.shape[1:], lambda b, s: (idx,) + (0,) * (arr.ndim - 1),
                            pipeline_mode=pl.Buffered(1))

    def hist_rows(k, frames=t):
        return ROW_PITCH * bb * (k - 1 + frames)

    state_dims = ((POOL_STATE, W_POOL), (SCONV_K - 1, W_SCONV), (CCONV_K - 1, W_CCONV),
                  (FFN_CONV_K - 1, 2 * D_FF))
    in_specs = [act_spec(D_MODEL),
                pl.BlockSpec((None, bb, t, D_PLE), lambda b, s: (layer, b, s, 0))]
    in_specs += [stacked_state_spec(r, c) for r, c in state_dims]
    in_specs += [weight_spec(a) for a in weights]
    out_specs = [act_spec(D_MODEL)] + [state_spec(r, c) for r, c in state_dims]
    out_shape = [jax.ShapeDtypeStruct((batch, seq, D_MODEL), f32)]
    out_shape += [jax.ShapeDtypeStruct((batch, r, c), f32) for r, c in state_dims]
    scratch = [pltpu.VMEM((W_POOL // V7X_LANES, hist_rows(POOL_STATE + 1), V7X_LANES), f32),
               pltpu.VMEM((W_SCONV // V7X_LANES, hist_rows(SCONV_K), V7X_LANES), f32),
               pltpu.VMEM((W_CCONV // V7X_LANES, hist_rows(CCONV_K), V7X_LANES), f32),
               pltpu.VMEM((2, 2, FFN_SLABS, hist_rows(FFN_CONV_K, min(t, SUB_ROWS)), V7X_LANES),
                          f32),
               pltpu.VMEM((bb, FFN_CONV_K - 1, 2 * D_FF), f32),
               pltpu.VMEM((m, D_MODEL), bf16),
               pltpu.VMEM((m, D_MODEL), bf16),
               pltpu.VMEM((m, D_MODEL), f32),
               pltpu.VMEM((m, W_SCONV), f32),
               pltpu.VMEM((m, W_POOL), bf16),
               pltpu.VMEM((m, D_MODEL), bf16),
               pltpu.VMEM((m, D_FF), bf16)]

    kern = functools.partial(_layer_kernel, bb=bb, t=t, n_seq=n_seq, pos0=pos0, final=final)
    return pl.pallas_call(
        kern, grid=grid, in_specs=in_specs, out_specs=out_specs, out_shape=out_shape,
        scratch_shapes=scratch,
        compiler_params=pltpu.CompilerParams(
            dimension_semantics=("arbitrary", "arbitrary"),
            vmem_limit_bytes=V7X_VMEM_BYTES - 8 * 1024 * 1024),
        name="trunk_layer_final" if final else "trunk_layer",
    )(h, p, *states, *weights)


def _stacked_weights(g_mix, w_in, w_pool, pool_scale, sconv_w, sconv_b, cconv_w, cconv_b,
                     ln_g, ln_b, w_out, g_ffn, w_up, ffn_conv_w, ffn_conv_b, w_down, g_ple,
                     w_ple_gate, w_ple_proj, g_final):
    bf16 = jnp.bfloat16
    rows = lambda v: v[:, None, :]
    n_groups = len(POOL_WINDOWS)
    eye = jnp.eye(n_groups, dtype=w_pool.dtype)
    wp = jnp.einsum("lgcd,gh->lgchd", w_pool, eye).reshape(-1, W_POOL, W_POOL)
    return (rows(g_mix), w_in.astype(bf16), wp.astype(bf16), rows(pool_scale),
            sconv_w, rows(sconv_b), cconv_w, rows(cconv_b), rows(ln_g), rows(ln_b),
            w_out.astype(bf16), rows(g_ffn), w_up.astype(bf16), ffn_conv_w, rows(ffn_conv_b),
            w_down.astype(bf16), rows(g_ple), w_ple_gate.astype(bf16), w_ple_proj.astype(bf16),
            g_final[None, None, :])


def kernel(x_prompt, x_sample, p_prompt, p_sample, state_pool, state_sconv, state_cconv, state_ffn, g_mix, w_in, w_pool, pool_scale, sconv_w, sconv_b, cconv_w, cconv_b, ln_g, ln_b, w_out, g_ffn, w_up, ffn_conv_w, ffn_conv_b, w_down, g_ple, w_ple_gate, w_ple_proj, g_final):
    depth = w_in.shape[0]
    batch = x_prompt.shape[0]
    dt = x_prompt.dtype
    zero_states = (jnp.zeros((1, batch, POOL_STATE, W_POOL), dt),
                   jnp.zeros((1, batch, SCONV_K - 1, W_SCONV), dt),
                   jnp.zeros((1, batch, CCONV_K - 1, W_CCONV), dt),
                   jnp.zeros((1, batch, FFN_CONV_K - 1, 2 * D_FF), dt))
    sample_states_in = (state_pool, state_sconv, state_cconv, state_ffn)
    weights = _stacked_weights(g_mix, w_in, w_pool, pool_scale, sconv_w, sconv_b, cconv_w,
                               cconv_b, ln_g, ln_b, w_out, g_ffn, w_up, ffn_conv_w, ffn_conv_b,
                               w_down, g_ple, w_ple_gate, w_ple_proj, g_final)
    hp, hs = x_prompt, x_sample
    prompt_states, sample_states = [], []
    for i in range(depth):
        final = i == depth - 1
        hp, *st = _layer(hp, p_prompt, zero_states, weights, i, 0, 0, final)
        prompt_states.append(st)
        hs, *st = _layer(hs, p_sample, sample_states_in, weights, i, i, PAST_LEN, final)
        sample_states.append(st)
    stack = lambda sts, k: jnp.stack([st[k] for st in sts])
    return (hp, hs,
            stack(prompt_states, 0), stack(prompt_states, 1), stack(prompt_states, 2),
            stack(prompt_states, 3),
            stack(sample_states, 0), stack(sample_states, 1), stack(sample_states, 2),
            stack(sample_states, 3))
```

```python
import collections
import functools

import jax
import jax.numpy as jnp
from jax import lax
from jax.experimental import pallas as pl
from jax.experimental.pallas import tpu as pltpu

D_MODEL = 1024
W_POOL = 256
POOL_WINDOWS = (2, 4, 8, 16)
POOL_GROUP = W_POOL // len(POOL_WINDOWS)
POOL_STATE = max(POOL_WINDOWS) - 1
W_SCONV = 384
W_CCONV = 384
SCONV_K = 3
CCONV_K = 31
D_FF = 2816
FFN_CONV_K = 3
D_PLE = 256
PAST_LEN = 2048
EPS = 1e-6
D_IN = W_POOL + 3 * W_SCONV + 2 * W_CCONV

O_B = W_POOL
O_C = O_B + W_SCONV
O_V = O_C + W_SCONV
O_GA = O_V + W_SCONV
O_GG = O_GA + W_CCONV

V7X_LANES = 128
V7X_BF16_ROWS = 16
V7X_MXU_DIM = 256
V7X_VMEM_BYTES = 64 * 1024 * 1024

ROW_PITCH = 1
FFN_CHUNK = V7X_MXU_DIM
N_FFN_CHUNKS = D_FF // FFN_CHUNK
FFN_SLABS = FFN_CHUNK // V7X_LANES
DOWN_GROUP = 4
ROW_CHUNK = 128
BLOCK_ROWS = 512
SHORT_BLOCK_ROWS = 256

Chunk = collections.namedtuple("Chunk", "r0 r1 pieces")


def _sigmoid(x):
    return 1.0 / (1.0 + jnp.exp(-x))


def _rms(x, g_ref):
    ms = jnp.mean(x * x, axis=-1, keepdims=True)
    return x * lax.rsqrt(ms + EPS) * g_ref[...]


def _dot(a, b):
    return jnp.dot(a, b, preferred_element_type=jnp.float32)


def _cols(j):
    return slice(j * V7X_LANES, (j + 1) * V7X_LANES)


class _History:
    def __init__(self, ref, hist, t):
        self.ref, self.hist, self.t = ref, hist, t

    def _rows(self, b, frame, n):
        first = b * (self.hist + self.t) + self.hist + frame
        return pl.ds(ROW_PITCH * first, n, stride=ROW_PITCH)

    def read(self, slab, b, frame, n):
        return self.ref[slab, self._rows(b, frame, n), :]

    def write(self, slab, b, frame, val):
        self.ref[slab, self._rows(b, frame, val.shape[0]), :] = val

    def window(self, slab, chunk, back):
        parts = [self.read(slab, b, f0 - back, n) for b, f0, n in chunk.pieces]
        return parts[0] if len(parts) == 1 else jnp.concatenate(parts, axis=0)

    def write_block(self, slab, bb, val):
        for b in range(bb):
            self.write(slab, b, 0, val[b * self.t:(b + 1) * self.t, :])

    def tail(self, slab, b):
        return self.read(slab, b, self.t - self.hist, self.hist)


def _layer_kernel(h_ref, p_ref, stp_ref, sts_ref, stc_ref, stf_ref,
                  gmix_ref, win_ref, wpool_ref, pscale_ref, scw_ref, scb_ref,
                  ccw_ref, ccb_ref, lng_ref, lnb_ref, wout_ref, gffn_ref, wup_ref,
                  fcw_ref, fcb_ref, wdown_ref, gple_ref, wgate_ref, wproj_ref, gfin_ref,
                  ho_ref, nsp_ref, nss_ref, nsc_ref, nsf_ref,
                  pool_buf, sconv_buf, cconv_buf, ffn_buf, ffn_carry,
                  norm_buf, res_buf, bg_buf, diff_buf, mix_buf, act_buf,
                  *, bb, t, n_seq, pos0, final):
    m = bb * t
    s = pl.program_id(1)
    f32, bf16 = jnp.float32, jnp.bfloat16

    if bb == 1:
        chunks = [Chunk(r, r + ROW_CHUNK, ((0, r, ROW_CHUNK),)) for r in range(0, t, ROW_CHUNK)]
    else:
        per = ROW_CHUNK // t
        chunks = [Chunk(b * t, (b + per) * t, tuple((b + i, 0, t) for i in range(per)))
                  for b in range(0, bb, per)]

    pool_h = _History(pool_buf, POOL_STATE, t)
    sconv_h = _History(sconv_buf, SCONV_K - 1, t)
    cconv_h = _History(cconv_buf, CCONV_K - 1, t)

    @pl.when(s == 0)
    def _():
        for b in range(bb):
            for j in range(W_POOL // V7X_LANES):
                pool_h.write(j, b, -POOL_STATE, stp_ref[b, :, _cols(j)])
            for j in range(W_SCONV // V7X_LANES):
                sconv_h.write(j, b, -(SCONV_K - 1), sts_ref[b, :, _cols(j)])
            for j in range(W_CCONV // V7X_LANES):
                cconv_h.write(j, b, -(CCONV_K - 1), stc_ref[b, :, _cols(j)])
        ffn_carry[...] = stf_ref[...]

    def rms_to_norm_buf(src_ref, g_ref):
        for ch in chunks:
            norm_buf[ch.r0:ch.r1, :] = _rms(src_ref[ch.r0:ch.r1, :], g_ref).astype(bf16)

    res_buf[...] = h_ref[...].reshape(m, D_MODEL)
    rms_to_norm_buf(res_buf, gmix_ref)
    n1 = norm_buf[...]
    u_c = _dot(n1, win_ref[:, O_GA:D_IN])
    glu = u_c[:, 0:W_CCONV] * _sigmoid(u_c[:, W_CCONV:2 * W_CCONV])
    for j in range(W_CCONV // V7X_LANES):
        cconv_h.write_block(j, bb, glu[:, _cols(j)])
    u_ab = _dot(n1, win_ref[:, 0:O_GA])
    for j in range(W_POOL // V7X_LANES):
        pool_h.write_block(j, bb, u_ab[:, _cols(j)])
    bg_buf[...] = u_ab[:, O_B:O_C]
    cv = u_ab[:, O_C:O_V] * u_ab[:, O_V:O_GA]
    for j in range(W_SCONV // V7X_LANES):
        sconv_h.write_block(j, bb, cv[:, _cols(j)])

    lane = lax.broadcasted_iota(jnp.int32, (1, V7X_LANES), 1)
    low_half = lane < POOL_GROUP

    for ch in chunks:
        rows = ch.r1 - ch.r0
        cc = []
        for j in range(W_CCONV // V7X_LANES):
            acc = None
            for k in range(CCONV_K):
                term = cconv_h.window(j, ch, CCONV_K - 1 - k) * ccw_ref[k:k + 1, _cols(j)]
                acc = term if acc is None else acc + term
            cc.append(acc + ccb_ref[:, _cols(j)])
        mu = jnp.sum(cc[0] + cc[1] + cc[2], axis=-1, keepdims=True) * (1.0 / W_CCONV)
        xc = [c - mu for c in cc]
        var = jnp.sum(xc[0] * xc[0] + xc[1] * xc[1] + xc[2] * xc[2], axis=-1,
                      keepdims=True) * (1.0 / W_CCONV)
        rstd = lax.rsqrt(var + EPS)
        for j in range(W_CCONV // V7X_LANES):
            ln = xc[j] * rstd * lng_ref[:, _cols(j)] + lnb_ref[:, _cols(j)]
            c0 = O_B + W_SCONV + j * V7X_LANES
            mix_buf[ch.r0:ch.r1, c0:c0 + V7X_LANES] = (ln * _sigmoid(ln)).astype(bf16)

        frame = lax.broadcasted_iota(jnp.int32, (rows, V7X_LANES), 0)
        frame = lax.rem(frame, t) if bb > 1 else frame + ch.r0
        avail = pos0 + s * t + frame + 1
        for j, (w_lo, w_hi) in enumerate(((2, 4), (8, 16))):
            x_now = pool_h.window(j, ch, 0)
            acc = x_now
            sum_lo = None
            for k in range(1, w_hi):
                acc = acc + pool_h.window(j, ch, k)
                if k == w_lo - 1:
                    sum_lo = acc
            ssum = jnp.where(low_half, sum_lo, acc)
            cnt = jnp.minimum(avail, jnp.where(low_half, w_lo, w_hi)).astype(f32)
            diff_buf[ch.r0:ch.r1, _cols(j)] = (ssum / cnt - x_now).astype(bf16)

        for j in range(W_SCONV // V7X_LANES):
            acc = None
            for k in range(SCONV_K):
                term = sconv_h.window(j, ch, SCONV_K - 1 - k) * scw_ref[k:k + 1, _cols(j)]
                acc = term if acc is None else acc + term
            y = bg_buf[ch.r0:ch.r1, _cols(j)] * (acc + scb_ref[:, _cols(j)])
            c0 = O_B + j * V7X_LANES
            mix_buf[ch.r0:ch.r1, c0:c0 + V7X_LANES] = y.astype(bf16)

    for hist, out_ref, width in ((pool_h, nsp_ref, W_POOL), (sconv_h, nss_ref, W_SCONV),
                                 (cconv_h, nsc_ref, W_CCONV)):
        for b in range(bb):
            for j in range(width // V7X_LANES):
                tail = hist.tail(j, b)
                out_ref[b, :, _cols(j)] = tail
                if n_seq > 1:
                    hist.write(j, b, -hist.hist, tail)

    mix_buf[:, 0:O_B] = (_dot(diff_buf[...], wpool_ref[...]) * pscale_ref[...]).astype(bf16)
    res_buf[...] = res_buf[...] + _dot(mix_buf[...], wout_ref[...])

    rms_to_norm_buf(res_buf, gffn_ref)
    n2 = norm_buf[...]

    def up_chunk(j):
        hists = []
        for gv, c0 in enumerate((j * FFN_CHUNK, D_FF + j * FFN_CHUNK)):
            up = _dot(n2, wup_ref[:, c0:c0 + FFN_CHUNK])
            hist = _History(ffn_buf.at[j % 2, gv], FFN_CONV_K - 1, t)
            hists.append(hist)
            for q in range(FFN_SLABS):
                cq = slice(c0 + q * V7X_LANES, c0 + (q + 1) * V7X_LANES)
                hist.write_block(q, bb, up[:, _cols(q)])
                for b in range(bb):
                    hist.write(q, b, -hist.hist, ffn_carry[b, :, cq])
                    tail = hist.tail(q, b)
                    nsf_ref[b, :, cq] = tail
                    if n_seq > 1:
                        ffn_carry[b, :, cq] = tail
        return hists

    def conv_chunk(j, hists):
        for ch in chunks:
            for q in range(FFN_SLABS):
                conv = []
                for gv, c0 in enumerate((j * FFN_CHUNK, D_FF + j * FFN_CHUNK)):
                    cq = slice(c0 + q * V7X_LANES, c0 + (q + 1) * V7X_LANES)
                    acc = None
                    for k in range(FFN_CONV_K):
                        term = hists[gv].window(q, ch, FFN_CONV_K - 1 - k) * fcw_ref[k:k + 1, cq]
                        acc = term if acc is None else acc + term
                    conv.append(acc + fcb_ref[:, cq])
                gate, val = conv
                c0 = j * FFN_CHUNK + q * V7X_LANES
                act_buf[ch.r0:ch.r1, c0:c0 + V7X_LANES] = (
                    gate * _sigmoid(gate) * val).astype(bf16)

    pending = up_chunk(0)
    for j in range(N_FFN_CHUNKS):
        following = up_chunk(j + 1) if j + 1 < N_FFN_CHUNKS else None
        conv_chunk(j, pending)
        pending = following
        if (j + 1) % DOWN_GROUP == 0 or j + 1 == N_FFN_CHUNKS:
            k0, k1 = (j // DOWN_GROUP) * DOWN_GROUP * FFN_CHUNK, (j + 1) * FFN_CHUNK
            res_buf[...] = res_buf[...] + _dot(act_buf[:, k0:k1], wdown_ref[k0:k1, :])

    rms_to_norm_buf(res_buf, gple_ref)
    gate = _sigmoid(_dot(norm_buf[...], wgate_ref[...]))
    proj = _dot(p_ref[...].reshape(m, D_PLE).astype(bf16), wproj_ref[...])
    h3 = res_buf[...] + proj * gate
    if final:
        h3 = _rms(h3, gfin_ref)
    ho_ref[...] = h3.reshape(bb, t, D_MODEL)


def _block_sizes(batch, seq):
    if seq >= BLOCK_ROWS:
        assert seq % BLOCK_ROWS == 0
        return 1, BLOCK_ROWS
    assert ROW_CHUNK % seq == 0 and seq % V7X_BF16_ROWS == 0
    bb = min(batch, SHORT_BLOCK_ROWS // seq)
    assert batch % bb == 0 and (bb * seq) % ROW_CHUNK == 0
    return bb, seq


def _layer(h, p, states, weights, layer, state_layer, pos0, final):
    batch, seq, _ = h.shape
    bb, t = _block_sizes(batch, seq)
    m = bb * t
    n_seq = seq // t
    grid = (batch // bb, n_seq)
    f32, bf16 = jnp.float32, jnp.bfloat16

    def act_spec(c):
        return pl.BlockSpec((bb, t, c), lambda b, s: (b, s, 0))

    def state_spec(r, c):
        return pl.BlockSpec((bb, r, c), lambda b, s: (b, 0, 0))

    def stacked_state_spec(r, c):
        return pl.BlockSpec((None, bb, r, c), lambda b, s: (state_layer, b, 0, 0))

    def weight_spec(arr):
        idx = layer if arr.shape[0] > 1 else 0
        return pl.BlockSpec((None,) + arr.shape[1:], lambda b, s: (idx,) + (0,) * (arr.ndim - 1),
                            pipeline_mode=pl.Buffered(1))

    def hist_rows(k):
        return ROW_PITCH * bb * (k - 1 + t)

    state_dims = ((POOL_STATE, W_POOL), (SCONV_K - 1, W_SCONV), (CCONV_K - 1, W_CCONV),
                  (FFN_CONV_K - 1, 2 * D_FF))
    in_specs = [act_spec(D_MODEL),
                pl.BlockSpec((None, bb, t, D_PLE), lambda b, s: (layer, b, s, 0))]
    in_specs += [stacked_state_spec(r, c) for r, c in state_dims]
    in_specs += [weight_spec(a) for a in weights]
    out_specs = [act_spec(D_MODEL)] + [state_spec(r, c) for r, c in state_dims]
    out_shape = [jax.ShapeDtypeStruct((batch, seq, D_MODEL), f32)]
    out_shape += [jax.ShapeDtypeStruct((batch, r, c), f32) for r, c in state_dims]
    scratch = [pltpu.VMEM((W_POOL // V7X_LANES, hist_rows(POOL_STATE + 1), V7X_LANES), f32),
               pltpu.VMEM((W_SCONV // V7X_LANES, hist_rows(SCONV_K), V7X_LANES), f32),
               pltpu.VMEM((W_CCONV // V7X_LANES, hist_rows(CCONV_K), V7X_LANES), f32),
               pltpu.VMEM((2, 2, FFN_SLABS, hist_rows(FFN_CONV_K), V7X_LANES), f32),
               pltpu.VMEM((bb, FFN_CONV_K - 1, 2 * D_FF), f32),
               pltpu.VMEM((m, D_MODEL), bf16),
               pltpu.VMEM((m, D_MODEL), f32),
               pltpu.VMEM((m, W_SCONV), f32),
               pltpu.VMEM((m, W_POOL), bf16),
               pltpu.VMEM((m, D_MODEL), bf16),
               pltpu.VMEM((m, D_FF), bf16)]

    kern = functools.partial(_layer_kernel, bb=bb, t=t, n_seq=n_seq, pos0=pos0, final=final)
    return pl.pallas_call(
        kern, grid=grid, in_specs=in_specs, out_specs=out_specs, out_shape=out_shape,
        scratch_shapes=scratch,
        compiler_params=pltpu.CompilerParams(
            dimension_semantics=("arbitrary", "arbitrary"),
            vmem_limit_bytes=V7X_VMEM_BYTES - 8 * 1024 * 1024),
        name="trunk_layer_final" if final else "trunk_layer",
    )(h, p, *states, *weights)


def _stacked_weights(g_mix, w_in, w_pool, pool_scale, sconv_w, sconv_b, cconv_w, cconv_b,
                     ln_g, ln_b, w_out, g_ffn, w_up, ffn_conv_w, ffn_conv_b, w_down, g_ple,
                     w_ple_gate, w_ple_proj, g_final):
    bf16 = jnp.bfloat16
    rows = lambda v: v[:, None, :]
    n_groups = len(POOL_WINDOWS)
    eye = jnp.eye(n_groups, dtype=w_pool.dtype)
    wp = jnp.einsum("lgcd,gh->lgchd", w_pool, eye).reshape(-1, W_POOL, W_POOL)
    return (rows(g_mix), w_in.astype(bf16), wp.astype(bf16), rows(pool_scale),
            sconv_w, rows(sconv_b), cconv_w, rows(cconv_b), rows(ln_g), rows(ln_b),
            w_out.astype(bf16), rows(g_ffn), w_up.astype(bf16), ffn_conv_w, rows(ffn_conv_b),
            w_down.astype(bf16), rows(g_ple), w_ple_gate.astype(bf16), w_ple_proj.astype(bf16),
            g_final[None, None, :])


def kernel(x_prompt, x_sample, p_prompt, p_sample, state_pool, state_sconv, state_cconv, state_ffn, g_mix, w_in, w_pool, pool_scale, sconv_w, sconv_b, cconv_w, cconv_b, ln_g, ln_b, w_out, g_ffn, w_up, ffn_conv_w, ffn_conv_b, w_down, g_ple, w_ple_gate, w_ple_proj, g_final):
    depth = w_in.shape[0]
    batch = x_prompt.shape[0]
    dt = x_prompt.dtype
    zero_states = (jnp.zeros((1, batch, POOL_STATE, W_POOL), dt),
                   jnp.zeros((1, batch, SCONV_K - 1, W_SCONV), dt),
                   jnp.zeros((1, batch, CCONV_K - 1, W_CCONV), dt),
                   jnp.zeros((1, batch, FFN_CONV_K - 1, 2 * D_FF), dt))
    sample_states_in = (state_pool, state_sconv, state_cconv, state_ffn)
    weights = _stacked_weights(g_mix, w_in, w_pool, pool_scale, sconv_w, sconv_b, cconv_w,
                               cconv_b, ln_g, ln_b, w_out, g_ffn, w_up, ffn_conv_w, ffn_conv_b,
                               w_down, g_ple, w_ple_gate, w_ple_proj, g_final)
    hp, hs = x_prompt, x_sample
    prompt_states, sample_states = [], []
    for i in range(depth):
        final = i == depth - 1
        hp, *st = _layer(hp, p_prompt, zero_states, weights, i, 0, 0, final)
        prompt_states.append(st)
        hs, *st = _layer(hs, p_sample, sample_states_in, weights, i, i, PAST_LEN, final)
        sample_states.append(st)
    stack = lambda sts, k: jnp.stack([st[k] for st in sts])
    return (hp, hs,
            stack(prompt_states, 0), stack(prompt_states, 1), stack(prompt_states, 2),
            stack(prompt_states, 3),
            stack(sample_states, 0), stack(sample_states, 1), stack(sample_states, 2),
            stack(sample_states, 3))
```

```python
import collections
import functools

import jax
import jax.numpy as jnp
from jax import lax
from jax.experimental import pallas as pl
from jax.experimental.pallas import tpu as pltpu

D_MODEL = 1024
W_POOL = 256
POOL_WINDOWS = (2, 4, 8, 16)
POOL_GROUP = W_POOL // len(POOL_WINDOWS)
POOL_STATE = max(POOL_WINDOWS) - 1
W_SCONV = 384
W_CCONV = 384
SCONV_K = 3
CCONV_K = 31
D_FF = 2816
FFN_CONV_K = 3
D_PLE = 256
PAST_LEN = 2048
EPS = 1e-6
D_IN = W_POOL + 3 * W_SCONV + 2 * W_CCONV

O_B = W_POOL
O_C = O_B + W_SCONV
O_V = O_C + W_SCONV
O_GA = O_V + W_SCONV
O_GG = O_GA + W_CCONV

V7X_LANES = 128
V7X_BF16_ROWS = 16
V7X_MXU_DIM = 256
V7X_VMEM_BYTES = 64 * 1024 * 1024

ROW_PITCH = 1
FFN_CHUNK = V7X_MXU_DIM
N_FFN_CHUNKS = D_FF // FFN_CHUNK
FFN_SLABS = FFN_CHUNK // V7X_LANES
DOWN_GROUP = N_FFN_CHUNKS
ROW_CHUNK = 128
BLOCK_ROWS = 512
SHORT_BLOCK_ROWS = 256

Chunk = collections.namedtuple("Chunk", "r0 r1 pieces")


def _sigmoid(x):
    return 1.0 / (1.0 + jnp.exp(-x))


def _rms(x, g_ref):
    ms = jnp.mean(x * x, axis=-1, keepdims=True)
    return x * lax.rsqrt(ms + EPS) * g_ref[...]


def _dot(a, b):
    return jnp.dot(a, b, preferred_element_type=jnp.float32)


def _cols(j):
    return slice(j * V7X_LANES, (j + 1) * V7X_LANES)


class _History:
    def __init__(self, ref, hist, t):
        self.ref, self.hist, self.t = ref, hist, t

    def _rows(self, b, frame, n):
        first = b * (self.hist + self.t) + self.hist + frame
        return pl.ds(ROW_PITCH * first, n, stride=ROW_PITCH)

    def read(self, slab, b, frame, n):
        return self.ref[slab, self._rows(b, frame, n), :]

    def write(self, slab, b, frame, val):
        self.ref[slab, self._rows(b, frame, val.shape[0]), :] = val

    def window(self, slab, chunk, back):
        parts = [self.read(slab, b, f0 - back, n) for b, f0, n in chunk.pieces]
        return parts[0] if len(parts) == 1 else jnp.concatenate(parts, axis=0)

    def write_block(self, slab, bb, val):
        for b in range(bb):
            self.write(slab, b, 0, val[b * self.t:(b + 1) * self.t, :])

    def tail(self, slab, b):
        return self.read(slab, b, self.t - self.hist, self.hist)


def _layer_kernel(h_ref, p_ref, stp_ref, sts_ref, stc_ref, stf_ref,
                  gmix_ref, win_ref, wpool_ref, pscale_ref, scw_ref, scb_ref,
                  ccw_ref, ccb_ref, lng_ref, lnb_ref, wout_ref, gffn_ref, wup_ref,
                  fcw_ref, fcb_ref, wdown_ref, gple_ref, wgate_ref, wproj_ref, gfin_ref,
                  ho_ref, nsp_ref, nss_ref, nsc_ref, nsf_ref,
                  pool_buf, sconv_buf, cconv_buf, ffn_buf, ffn_carry,
                  norm_buf, res_buf, bg_buf, diff_buf, mix_buf, act_buf,
                  *, bb, t, n_seq, pos0, final):
    m = bb * t
    s = pl.program_id(1)
    f32, bf16 = jnp.float32, jnp.bfloat16

    if bb == 1:
        chunks = [Chunk(r, r + ROW_CHUNK, ((0, r, ROW_CHUNK),)) for r in range(0, t, ROW_CHUNK)]
    else:
        per = ROW_CHUNK // t
        chunks = [Chunk(b * t, (b + per) * t, tuple((b + i, 0, t) for i in range(per)))
                  for b in range(0, bb, per)]

    pool_h = _History(pool_buf, POOL_STATE, t)
    sconv_h = _History(sconv_buf, SCONV_K - 1, t)
    cconv_h = _History(cconv_buf, CCONV_K - 1, t)

    @pl.when(s == 0)
    def _():
        for b in range(bb):
            for j in range(W_POOL // V7X_LANES):
                pool_h.write(j, b, -POOL_STATE, stp_ref[b, :, _cols(j)])
            for j in range(W_SCONV // V7X_LANES):
                sconv_h.write(j, b, -(SCONV_K - 1), sts_ref[b, :, _cols(j)])
            for j in range(W_CCONV // V7X_LANES):
                cconv_h.write(j, b, -(CCONV_K - 1), stc_ref[b, :, _cols(j)])
        ffn_carry[...] = stf_ref[...]

    def rms_to_norm_buf(src_ref, g_ref):
        for ch in chunks:
            norm_buf[ch.r0:ch.r1, :] = _rms(src_ref[ch.r0:ch.r1, :], g_ref).astype(bf16)

    res_buf[...] = h_ref[...].reshape(m, D_MODEL)
    rms_to_norm_buf(res_buf, gmix_ref)
    n1 = norm_buf[...]
    u_c = _dot(n1, win_ref[:, O_GA:D_IN])
    glu = u_c[:, 0:W_CCONV] * _sigmoid(u_c[:, W_CCONV:2 * W_CCONV])
    for j in range(W_CCONV // V7X_LANES):
        cconv_h.write_block(j, bb, glu[:, _cols(j)])
    u_ab = _dot(n1, win_ref[:, 0:O_GA])
    for j in range(W_POOL // V7X_LANES):
        pool_h.write_block(j, bb, u_ab[:, _cols(j)])
    bg_buf[...] = u_ab[:, O_B:O_C]
    cv = u_ab[:, O_C:O_V] * u_ab[:, O_V:O_GA]
    for j in range(W_SCONV // V7X_LANES):
        sconv_h.write_block(j, bb, cv[:, _cols(j)])

    lane = lax.broadcasted_iota(jnp.int32, (1, V7X_LANES), 1)
    low_half = lane < POOL_GROUP

    for ch in chunks:
        rows = ch.r1 - ch.r0
        cc = []
        for j in range(W_CCONV // V7X_LANES):
            acc = None
            for k in range(CCONV_K):
                term = cconv_h.window(j, ch, CCONV_K - 1 - k) * ccw_ref[k:k + 1, _cols(j)]
                acc = term if acc is None else acc + term
            cc.append(acc + ccb_ref[:, _cols(j)])
        mu = jnp.sum(cc[0] + cc[1] + cc[2], axis=-1, keepdims=True) * (1.0 / W_CCONV)
        xc = [c - mu for c in cc]
        var = jnp.sum(xc[0] * xc[0] + xc[1] * xc[1] + xc[2] * xc[2], axis=-1,
                      keepdims=True) * (1.0 / W_CCONV)
        rstd = lax.rsqrt(var + EPS)
        for j in range(W_CCONV // V7X_LANES):
            ln = xc[j] * rstd * lng_ref[:, _cols(j)] + lnb_ref[:, _cols(j)]
            c0 = O_B + W_SCONV + j * V7X_LANES
            mix_buf[ch.r0:ch.r1, c0:c0 + V7X_LANES] = (ln * _sigmoid(ln)).astype(bf16)

        frame = lax.broadcasted_iota(jnp.int32, (rows, V7X_LANES), 0)
        frame = lax.rem(frame, t) if bb > 1 else frame + ch.r0
        avail = pos0 + s * t + frame + 1
        for j, (w_lo, w_hi) in enumerate(((2, 4), (8, 16))):
            x_now = pool_h.window(j, ch, 0)
            acc = x_now
            sum_lo = None
            for k in range(1, w_hi):
                acc = acc + pool_h.window(j, ch, k)
                if k == w_lo - 1:
                    sum_lo = acc
            ssum = jnp.where(low_half, sum_lo, acc)
            cnt = jnp.minimum(avail, jnp.where(low_half, w_lo, w_hi)).astype(f32)
            diff_buf[ch.r0:ch.r1, _cols(j)] = (ssum / cnt - x_now).astype(bf16)

        for j in range(W_SCONV // V7X_LANES):
            acc = None
            for k in range(SCONV_K):
                term = sconv_h.window(j, ch, SCONV_K - 1 - k) * scw_ref[k:k + 1, _cols(j)]
                acc = term if acc is None else acc + term
            y = bg_buf[ch.r0:ch.r1, _cols(j)] * (acc + scb_ref[:, _cols(j)])
            c0 = O_B + j * V7X_LANES
            mix_buf[ch.r0:ch.r1, c0:c0 + V7X_LANES] = y.astype(bf16)

    for hist, out_ref, width in ((pool_h, nsp_ref, W_POOL), (sconv_h, nss_ref, W_SCONV),
                                 (cconv_h, nsc_ref, W_CCONV)):
        for b in range(bb):
            for j in range(width // V7X_LANES):
                tail = hist.tail(j, b)
                out_ref[b, :, _cols(j)] = tail
                if n_seq > 1:
                    hist.write(j, b, -hist.hist, tail)

    mix_buf[:, 0:O_B] = (_dot(diff_buf[...], wpool_ref[...]) * pscale_ref[...]).astype(bf16)
    res_buf[...] = res_buf[...] + _dot(mix_buf[...], wout_ref[...])

    rms_to_norm_buf(res_buf, gffn_ref)
    n2 = norm_buf[...]

    def up_chunk(j):
        hists = []
        for gv, c0 in enumerate((j * FFN_CHUNK, D_FF + j * FFN_CHUNK)):
            up = _dot(n2, wup_ref[:, c0:c0 + FFN_CHUNK])
            hist = _History(ffn_buf.at[j % 2, gv], FFN_CONV_K - 1, t)
            hists.append(hist)
            for q in range(FFN_SLABS):
                cq = slice(c0 + q * V7X_LANES, c0 + (q + 1) * V7X_LANES)
                hist.write_block(q, bb, up[:, _cols(q)])
                for b in range(bb):
                    hist.write(q, b, -hist.hist, ffn_carry[b, :, cq])
                    tail = hist.tail(q, b)
                    nsf_ref[b, :, cq] = tail
                    if n_seq > 1:
                        ffn_carry[b, :, cq] = tail
        return hists

    def conv_chunk(j, hists):
        for ch in chunks:
            for q in range(FFN_SLABS):
                conv = []
                for gv, c0 in enumerate((j * FFN_CHUNK, D_FF + j * FFN_CHUNK)):
                    cq = slice(c0 + q * V7X_LANES, c0 + (q + 1) * V7X_LANES)
                    acc = None
                    for k in range(FFN_CONV_K):
                        term = hists[gv].window(q, ch, FFN_CONV_K - 1 - k) * fcw_ref[k:k + 1, cq]
                        acc = term if acc is None else acc + term
                    conv.append(acc + fcb_ref[:, cq])
                gate, val = conv
                c0 = j * FFN_CHUNK + q * V7X_LANES
                act_buf[ch.r0:ch.r1, c0:c0 + V7X_LANES] = (
                    gate * _sigmoid(gate) * val).astype(bf16)

    pending = up_chunk(0)
    for j in range(N_FFN_CHUNKS):
        following = up_chunk(j + 1) if j + 1 < N_FFN_CHUNKS else None
        conv_chunk(j, pending)
        pending = following
        if (j + 1) % DOWN_GROUP == 0 or j + 1 == N_FFN_CHUNKS:
            k0, k1 = (j // DOWN_GROUP) * DOWN_GROUP * FFN_CHUNK, (j + 1) * FFN_CHUNK
            res_buf[...] = res_buf[...] + _dot(act_buf[:, k0:k1], wdown_ref[k0:k1, :])

    rms_to_norm_buf(res_buf, gple_ref)
    gate = _sigmoid(_dot(norm_buf[...], wgate_ref[...]))
    proj = _dot(p_ref[...].reshape(m, D_PLE).astype(bf16), wproj_ref[...])
    h3 = res_buf[...] + proj * gate
    if final:
        h3 = _rms(h3, gfin_ref)
    ho_ref[...] = h3.reshape(bb, t, D_MODEL)


def _block_sizes(batch, seq):
    if seq >= BLOCK_ROWS:
        assert seq % BLOCK_ROWS == 0
        return 1, BLOCK_ROWS
    assert ROW_CHUNK % seq == 0 and seq % V7X_BF16_ROWS == 0
    bb = min(batch, SHORT_BLOCK_ROWS // seq)
    assert batch % bb == 0 and (bb * seq) % ROW_CHUNK == 0
    return bb, seq


def _layer(h, p, states, weights, layer, state_layer, pos0, final):
    batch, seq, _ = h.shape
    bb, t = _block_sizes(batch, seq)
    m = bb * t
    n_seq = seq // t
    grid = (batch // bb, n_seq)
    f32, bf16 = jnp.float32, jnp.bfloat16

    def act_spec(c):
        return pl.BlockSpec((bb, t, c), lambda b, s: (b, s, 0))

    def state_spec(r, c):
        return pl.BlockSpec((bb, r, c), lambda b, s: (b, 0, 0))

    def stacked_state_spec(r, c):
        return pl.BlockSpec((None, bb, r, c), lambda b, s: (state_layer, b, 0, 0))

    def weight_spec(arr):
        idx = layer if arr.shape[0] > 1 else 0
        return pl.BlockSpec((None,) + arr.shape[1:], lambda b, s: (idx,) + (0,) * (arr.ndim - 1),
                            pipeline_mode=pl.Buffered(1))

    def hist_rows(k):
        return ROW_PITCH * bb * (k - 1 + t)

    state_dims = ((POOL_STATE, W_POOL), (SCONV_K - 1, W_SCONV), (CCONV_K - 1, W_CCONV),
                  (FFN_CONV_K - 1, 2 * D_FF))
    in_specs = [act_spec(D_MODEL),
                pl.BlockSpec((None, bb, t, D_PLE), lambda b, s: (layer, b, s, 0))]
    in_specs += [stacked_state_spec(r, c) for r, c in state_dims]
    in_specs += [weight_spec(a) for a in weights]
    out_specs = [act_spec(D_MODEL)] + [state_spec(r, c) for r, c in state_dims]
    out_shape = [jax.ShapeDtypeStruct((batch, seq, D_MODEL), f32)]
    out_shape += [jax.ShapeDtypeStruct((batch, r, c), f32) for r, c in state_dims]
    scratch = [pltpu.VMEM((W_POOL // V7X_LANES, hist_rows(POOL_STATE + 1), V7X_LANES), f32),
               pltpu.VMEM((W_SCONV // V7X_LANES, hist_rows(SCONV_K), V7X_LANES), f32),
               pltpu.VMEM((W_CCONV // V7X_LANES, hist_rows(CCONV_K), V7X_LANES), f32),
               pltpu.VMEM((2, 2, FFN_SLABS, hist_rows(FFN_CONV_K), V7X_LANES), f32),
               pltpu.VMEM((bb, FFN_CONV_K - 1, 2 * D_FF), f32),
               pltpu.VMEM((m, D_MODEL), bf16),
               pltpu.VMEM((m, D_MODEL), f32),
               pltpu.VMEM((m, W_SCONV), f32),
               pltpu.VMEM((m, W_POOL), bf16),
               pltpu.VMEM((m, D_MODEL), bf16),
               pltpu.VMEM((m, D_FF), bf16)]

    kern = functools.partial(_layer_kernel, bb=bb, t=t, n_seq=n_seq, pos0=pos0, final=final)
    return pl.pallas_call(
        kern, grid=grid, in_specs=in_specs, out_specs=out_specs, out_shape=out_shape,
        scratch_shapes=scratch,
        compiler_params=pltpu.CompilerParams(
            dimension_semantics=("arbitrary", "arbitrary"),
            vmem_limit_bytes=V7X_VMEM_BYTES - 8 * 1024 * 1024),
        name="trunk_layer_final" if final else "trunk_layer",
    )(h, p, *states, *weights)


def _stacked_weights(g_mix, w_in, w_pool, pool_scale, sconv_w, sconv_b, cconv_w, cconv_b,
                     ln_g, ln_b, w_out, g_ffn, w_up, ffn_conv_w, ffn_conv_b, w_down, g_ple,
                     w_ple_gate, w_ple_proj, g_final):
    bf16 = jnp.bfloat16
    rows = lambda v: v[:, None, :]
    n_groups = len(POOL_WINDOWS)
    eye = jnp.eye(n_groups, dtype=w_pool.dtype)
    wp = jnp.einsum("lgcd,gh->lgchd", w_pool, eye).reshape(-1, W_POOL, W_POOL)
    return (rows(g_mix), w_in.astype(bf16), wp.astype(bf16), rows(pool_scale),
            sconv_w, rows(sconv_b), cconv_w, rows(cconv_b), rows(ln_g), rows(ln_b),
            w_out.astype(bf16), rows(g_ffn), w_up.astype(bf16), ffn_conv_w, rows(ffn_conv_b),
            w_down.astype(bf16), rows(g_ple), w_ple_gate.astype(bf16), w_ple_proj.astype(bf16),
            g_final[None, None, :])


def kernel(x_prompt, x_sample, p_prompt, p_sample, state_pool, state_sconv, state_cconv, state_ffn, g_mix, w_in, w_pool, pool_scale, sconv_w, sconv_b, cconv_w, cconv_b, ln_g, ln_b, w_out, g_ffn, w_up, ffn_conv_w, ffn_conv_b, w_down, g_ple, w_ple_gate, w_ple_proj, g_final):
    depth = w_in.shape[0]
    batch = x_prompt.shape[0]
    dt = x_prompt.dtype
    zero_states = (jnp.zeros((1, batch, POOL_STATE, W_POOL), dt),
                   jnp.zeros((1, batch, SCONV_K - 1, W_SCONV), dt),
                   jnp.zeros((1, batch, CCONV_K - 1, W_CCONV), dt),
                   jnp.zeros((1, batch, FFN_CONV_K - 1, 2 * D_FF), dt))
    sample_states_in = (state_pool, state_sconv, state_cconv, state_ffn)
    weights = _stacked_weights(g_mix, w_in, w_pool, pool_scale, sconv_w, sconv_b, cconv_w,
                               cconv_b, ln_g, ln_b, w_out, g_ffn, w_up, ffn_conv_w, ffn_conv_b,
                               w_down, g_ple, w_ple_gate, w_ple_proj, g_final)
    hp, hs = x_prompt, x_sample
    prompt_states, sample_states = [], []
    for i in range(depth):
        final = i == depth - 1
        hp, *st = _layer(hp, p_prompt, zero_states, weights, i, 0, 0, final)
        prompt_states.append(st)
        hs, *st = _layer(hs, p_sample, sample_states_in, weights, i, i, PAST_LEN, final)
        sample_states.append(st)
    stack = lambda sts, k: jnp.stack([st[k] for st in sts])
    return (hp, hs,
            stack(prompt_states, 0), stack(prompt_states, 1), stack(prompt_states, 2),
            stack(prompt_states, 3),
            stack(sample_states, 0), stack(sample_states, 1), stack(sample_states, 2),
            stack(sample_states, 3))
```
